```python
import math
import jax, jax.numpy as jnp
from jax import lax
import numpy as np

D_MODEL = 1024
BATCH = 32
SEQ = 2048
DEPTH = 1

CHUNK = 64
N_META = 16

N_HEADS = 8
HEAD_DIM = 64
V_HEAD_DIM = 2 * HEAD_DIM
QK_COLS = N_HEADS * 2 * HEAD_DIM
ATTN_WIDTH = N_HEADS * V_HEAD_DIM
ROPE_DIMS = HEAD_DIM // 4
ROPE_THETA = 500000.0
Q_BLOCK = 128
NEG_INF = -1e30

CONV_WIDTH = D_MODEL
CONV_K = 31

SPLIT_IDX = [QK_COLS, 2 * QK_COLS, 2 * QK_COLS + ATTN_WIDTH,
             2 * QK_COLS + ATTN_WIDTH + 2 * CONV_WIDTH]
IN_COLS = 2 * QK_COLS + ATTN_WIDTH + 2 * CONV_WIDTH + 2 * D_MODEL

PEER_HEADS = 8
PEER_KEYS = 128
PEER_N = PEER_KEYS * PEER_KEYS
PEER_DKEY = 256
PEER_TOPK = 16
PEER_TOKEN_BLOCK = 128

LN_EPS = 1e-5
DEEPNORM_ALPHA = (2 * DEPTH) ** 0.25
DEEPNORM_BETA = (8 * DEPTH) ** -0.25

kernel_name = "hybrid_diffattn_conformer_peer_block"


def layer_norm(x, g, b):
    xf = x.astype(jnp.float32)
    mu = jnp.mean(xf, axis=-1, keepdims=True)
    var = jnp.mean(jnp.square(xf - mu), axis=-1, keepdims=True)
    y = (xf - mu) * lax.rsqrt(var + LN_EPS)
    return (y * g.astype(jnp.float32) + b.astype(jnp.float32)).astype(x.dtype)


def chunk_ids(L):
    pos = np.arange(L)
    return np.where(pos < N_META, 0, (pos - N_META) // CHUNK + 1)


def chunk_end(p, L):
    if p < N_META:
        return N_META
    c = (p - N_META) // CHUNK + 1
    return min(L, N_META + CHUNK * c)


def rope_tables(L, dtype):
    pos = jnp.arange(L, dtype=jnp.float32)
    inv = ROPE_THETA ** (-jnp.arange(0, ROPE_DIMS, 2, dtype=jnp.float32) / ROPE_DIMS)
    ang = (pos[:, None] * inv[None, :])[None, :, None, None, :]
    return jnp.cos(ang).astype(dtype), jnp.sin(ang).astype(dtype)


def apply_partial_rope(t, cos, sin):
    rot, rest = t[..., :ROPE_DIMS], t[..., ROPE_DIMS:]
    r1, r2 = rot[..., :ROPE_DIMS // 2], rot[..., ROPE_DIMS // 2:]
    rot = jnp.concatenate([r1 * cos - r2 * sin, r2 * cos + r1 * sin], axis=-1)
    return jnp.concatenate([rot, rest], axis=-1)


def diff_attention(q, k, v, lam):
    L = q.shape[1]
    cid = chunk_ids(L)
    scale = HEAD_DIM ** -0.5
    outs = []
    for qs in range(0, L, Q_BLOCK):
        qe = min(qs + Q_BLOCK, L)
        kend = chunk_end(qe - 1, L)
        s = jnp.einsum('bqhmd,bkhmd->bhmqk', q[:, qs:qe], k[:, :kend]).astype(jnp.float32) * scale
        mask = jnp.asarray(cid[None, :kend] <= cid[qs:qe, None])
        p = jax.nn.softmax(jnp.where(mask, s, NEG_INF), axis=-1)
        w = p[:, :, 0] - lam * p[:, :, 1]
        outs.append(jnp.einsum('bhqk,bkhe->bqhe', w.astype(v.dtype), v[:, :kend]))
    return jnp.concatenate(outs, axis=1)


def conformer_conv(c, dw_kernel, dw_bias, ln_g, ln_b, w_proj, b_proj):
    a, gt = jnp.split(c, 2, axis=-1)
    u = a * jax.nn.sigmoid(gt)
    u = lax.conv_general_dilated(
        u, dw_kernel[:, None, :].astype(u.dtype), window_strides=(1,),
        padding=((CONV_K - 1, 0),), dimension_numbers=('NWC', 'WIO', 'NWC'),
        feature_group_count=CONV_WIDTH) + dw_bias
    u = jax.nn.silu(layer_norm(u, ln_g, ln_b))
    return u @ w_proj + b_proj


def peer_ffn(h, wq, subkeys, u_tab, v_tab):
    Bb, L, D = h.shape
    t = h.reshape(-1, D)
    n = t.shape[0]
    pad = (-n) % PEER_TOKEN_BLOCK
    blocks = jnp.pad(t, ((0, pad), (0, 0))).reshape(-1, PEER_TOKEN_BLOCK, D)
    K = PEER_TOPK

    def one_block(tb):
        q = (tb @ wq).reshape(PEER_TOKEN_BLOCK, PEER_HEADS, 2, PEER_DKEY // 2)
        s = jnp.einsum('thpc,hpnc->thpn', q, subkeys).astype(jnp.float32)
        sv, si = lax.top_k(s, K)
        cand = sv[:, :, 0, :, None] + sv[:, :, 1, None, :]
        cv, ci = lax.top_k(cand.reshape(PEER_TOKEN_BLOCK, PEER_HEADS, K * K), K)
        i1 = jnp.take_along_axis(si[:, :, 0], ci // K, axis=-1)
        i2 = jnp.take_along_axis(si[:, :, 1], ci % K, axis=-1)
        e = i1 * PEER_KEYS + i2
        gate = jax.nn.softmax(cv, axis=-1)
        act = jax.nn.gelu(jnp.einsum('td,thkd->thk', tb, u_tab[e]), approximate=False)
        coef = (gate * act.astype(jnp.float32)).astype(tb.dtype)
        return jnp.einsum('thk,thkd->td', coef, v_tab[e])

    out = lax.map(one_block, blocks).reshape(-1, D)[:n]
    return out.reshape(Bb, L, D)


def setup_inputs(seed: int = 0) -> dict:
    key = jax.random.key(seed)
    ks = jax.random.split(key, 28)

    def nrm(k, shape, scale):
        return jax.random.normal(k, shape, jnp.float32) * scale

    def gain(k, shape):
        return 1.0 + nrm(k, shape, 0.02)

    Dp = DEPTH
    col_scale = jnp.concatenate([
        jnp.ones((2 * QK_COLS,), jnp.float32),
        jnp.full((ATTN_WIDTH,), DEEPNORM_BETA, jnp.float32),
        jnp.ones((2 * CONV_WIDTH + 2 * D_MODEL,), jnp.float32)])
    return {
        "x": nrm(ks[0], (BATCH, SEQ, D_MODEL), 1.0),
        "meta": nrm(ks[1], (N_META, D_MODEL), 1.0),
        "ln_in_g": gain(ks[2], (D_MODEL,)),
        "ln_in_b": nrm(ks[3], (D_MODEL,), 0.02),
        "w_in": nrm(ks[4], (Dp, D_MODEL, IN_COLS), D_MODEL ** -0.5) * col_scale,
        "b_in": nrm(ks[5], (Dp, IN_COLS), 0.02),
        "lambda_q1": nrm(ks[6], (Dp, HEAD_DIM), 0.1),
        "lambda_k1": nrm(ks[7], (Dp, HEAD_DIM), 0.1),
        "lambda_q2": nrm(ks[8], (Dp, HEAD_DIM), 0.1),
        "lambda_k2": nrm(ks[9], (Dp, HEAD_DIM), 0.1),
        "subln_g": gain(ks[10], (Dp, V_HEAD_DIM)),
        "w_attn_proj": nrm(ks[11], (Dp, ATTN_WIDTH, D_MODEL), ATTN_WIDTH ** -0.5 * DEEPNORM_BETA),
        "dw_kernel": nrm(ks[12], (Dp, CONV_K, CONV_WIDTH), CONV_K ** -0.5),
        "dw_bias": nrm(ks[13], (Dp, CONV_WIDTH), 0.02),
        "conv_ln_g": gain(ks[14], (Dp, CONV_WIDTH)),
        "conv_ln_b": nrm(ks[15], (Dp, CONV_WIDTH), 0.02),
        "w_conv_proj": nrm(ks[16], (Dp, CONV_WIDTH, D_MODEL), CONV_WIDTH ** -0.5 * DEEPNORM_BETA),
        "b_conv_proj": nrm(ks[17], (Dp, D_MODEL), 0.02),
        "w_out": nrm(ks[18], (Dp, D_MODEL, D_MODEL), D_MODEL ** -0.5 * DEEPNORM_BETA),
        "b_out": nrm(ks[19], (Dp, D_MODEL), 0.02),
        "ln1_g": gain(ks[20], (Dp, D_MODEL)),
        "ln1_b": nrm(ks[21], (Dp, D_MODEL), 0.02),
        "peer_wq": nrm(ks[22], (Dp, D_MODEL, PEER_HEADS * PEER_DKEY), D_MODEL ** -0.5),
        "peer_subkeys": nrm(ks[23], (Dp, PEER_HEADS, 2, PEER_KEYS, PEER_DKEY // 2), (PEER_DKEY // 2) ** -0.5),
        "peer_u": nrm(ks[24], (Dp, PEER_N, D_MODEL), D_MODEL ** -0.5),
        "peer_v": nrm(ks[25], (Dp, PEER_N, D_MODEL), DEEPNORM_BETA * PEER_HEADS ** -0.5),
        "ln2_g": gain(ks[26], (Dp, D_MODEL)),
        "ln2_b": nrm(ks[27], (Dp, D_MODEL), 0.02),
    }


def reference(x, meta, ln_in_g, ln_in_b, w_in, b_in, lambda_q1, lambda_k1, lambda_q2,
              lambda_k2, subln_g, w_attn_proj, dw_kernel, dw_bias, conv_ln_g, conv_ln_b,
              w_conv_proj, b_conv_proj, w_out, b_out, ln1_g, ln1_b, peer_wq, peer_subkeys,
              peer_u, peer_v, ln2_g, ln2_b):
    B = x.shape[0]
    h = jnp.concatenate(
        [jnp.broadcast_to(meta[None].astype(x.dtype), (B, N_META, D_MODEL)), x], axis=1)
    h = layer_norm(h, ln_in_g, ln_in_b)
    L = h.shape[1]
    cos, sin = rope_tables(L, h.dtype)

    for l in range(DEPTH):
        lam_init = 0.8 - 0.6 * math.exp(-0.3 * l)
        proj = h @ w_in[l] + b_in[l]
        q, k, v, c, g = jnp.split(proj, SPLIT_IDX, axis=-1)
        q = apply_partial_rope(q.reshape(B, L, N_HEADS, 2, HEAD_DIM), cos, sin)
        k = apply_partial_rope(k.reshape(B, L, N_HEADS, 2, HEAD_DIM), cos, sin)
        v = v.reshape(B, L, N_HEADS, V_HEAD_DIM)
        lam = (jnp.exp(jnp.sum(lambda_q1[l].astype(jnp.float32) * lambda_k1[l].astype(jnp.float32)))
               - jnp.exp(jnp.sum(lambda_q2[l].astype(jnp.float32) * lambda_k2[l].astype(jnp.float32)))
               + lam_init)
        o = diff_attention(q, k, v, lam).astype(jnp.float32)
        o = (o * lax.rsqrt(jnp.mean(jnp.square(o), axis=-1, keepdims=True) + LN_EPS)
             * subln_g[l].astype(jnp.float32) * (1.0 - lam_init)).astype(h.dtype)
        attn_d = o.reshape(B, L, ATTN_WIDTH) @ w_attn_proj[l]
        conv_d = conformer_conv(c, dw_kernel[l], dw_bias[l], conv_ln_g[l], conv_ln_b[l],
                                w_conv_proj[l], b_conv_proj[l])
        g_attn, g_conv = jnp.split(jax.nn.sigmoid(g), 2, axis=-1)
        y = (g_attn * attn_d + g_conv * conv_d) @ w_out[l] + b_out[l]
        h = layer_norm(DEEPNORM_ALPHA * h + y, ln1_g[l], ln1_b[l])
        f = peer_ffn(h, peer_wq[l], peer_subkeys[l], peer_u[l], peer_v[l])
        h = layer_norm(DEEPNORM_ALPHA * h + f, ln2_g[l], ln2_b[l])

    return h[:, N_META:]
```

```python
import functools
import math

import jax
import jax.numpy as jnp
from jax import lax
from jax.experimental import pallas as pl
from jax.experimental.pallas import tpu as pltpu

D_MODEL = 1024
CHUNK = 64
N_META = 16
N_HEADS = 8
HEAD_DIM = 64
V_HEAD_DIM = 2 * HEAD_DIM
QK_COLS = N_HEADS * 2 * HEAD_DIM
ATTN_WIDTH = N_HEADS * V_HEAD_DIM
ROPE_DIMS = HEAD_DIM // 4
ROPE_THETA = 500000.0
NEG_INF = -1e30
CONV_WIDTH = D_MODEL
CONV_K = 31
PEER_HEADS = 8
PEER_KEYS = 128
PEER_N = PEER_KEYS * PEER_KEYS
PEER_DKEY = 256
PEER_TOPK = 16
LN_EPS = 1e-5
DEPTH = 1
DEEPNORM_ALPHA = (2 * DEPTH) ** 0.25
LAM_INIT = 0.8 - 0.6 * math.exp(-0.3 * 0)

LANES = 128
META_PAD = 128
VMEM_LIMIT = 56 * 1024 * 1024

ROW_TILE = 256
ATTN_TILE = 256
CONV_TILE = 256
CONV_HALO = 32
ROUTE_TILE = 256
PEER_TOK = 512
PEER_EXP = 1024

F32 = jnp.float32
BF16 = jnp.bfloat16


def _layer_norm(x, g, b):
    mu = jnp.mean(x, axis=-1, keepdims=True)
    xc = x - mu
    var = jnp.mean(xc * xc, axis=-1, keepdims=True)
    return xc * lax.rsqrt(var + LN_EPS) * g + b


def _params(sem, vmem=VMEM_LIMIT):
    return pltpu.CompilerParams(dimension_semantics=sem, vmem_limit_bytes=vmem)


def _in_proj_kernel(x_ref, g_ref, b_ref, w_ref, bias_ref, cos_ref, sa_ref, sb_ref,
                    q_ref, k_ref, vt_ref, u_ref, sg_ref):
    hb = _layer_norm(x_ref[...], g_ref[...], b_ref[...]).astype(BF16)

    def proj(c0, n):
        return (jnp.dot(hb, w_ref[:, c0:c0 + n], preferred_element_type=F32)
                + bias_ref[:, c0:c0 + n])

    cos, sa, sb = cos_ref[...], sa_ref[...], sb_ref[...]

    def rope_store(t, out_ref, scale):
        for hh in range(QK_COLS // LANES):
            th = t[:, hh * LANES:(hh + 1) * LANES]
            r = (th * cos + pltpu.roll(th, LANES - ROPE_DIMS // 2, 1) * sa
                 + pltpu.roll(th, ROPE_DIMS // 2, 1) * sb)
            out_ref[:, hh * LANES:(hh + 1) * LANES] = (r * scale).astype(BF16)

    rope_store(proj(0, QK_COLS), q_ref, HEAD_DIM ** -0.5)
    rope_store(proj(QK_COLS, QK_COLS), k_ref, 1.0)
    vt_ref[0] = proj(2 * QK_COLS, ATTN_WIDTH).T.astype(BF16)
    c0 = 2 * QK_COLS + ATTN_WIDTH
    a = proj(c0, CONV_WIDTH)
    gt = proj(c0 + CONV_WIDTH, CONV_WIDTH)
    u_ref[...] = a * jax.nn.sigmoid(gt)
    sg_ref[...] = jax.nn.sigmoid(proj(c0 + 2 * CONV_WIDTH, 2 * D_MODEL))


def _in_proj(rows, seq, tile, x2, ln_g, ln_b, w_bf, bias, cos_t, sa_t, sb_t):
    n_seq = seq // tile
    nb = rows // seq
    in_cols = w_bf.shape[1]
    row = lambda i: (i, 0)
    const = lambda i: (0, 0)
    tab = lambda i: (i % n_seq, 0)
    return pl.pallas_call(
        _in_proj_kernel,
        grid=(rows // tile,),
        in_specs=[
            pl.BlockSpec((tile, D_MODEL), row),
            pl.BlockSpec((1, D_MODEL), const),
            pl.BlockSpec((1, D_MODEL), const),
            pl.BlockSpec((D_MODEL, in_cols), const, pipeline_mode=pl.Buffered(1)),
            pl.BlockSpec((1, in_cols), const),
            pl.BlockSpec((tile, LANES), tab),
            pl.BlockSpec((tile, LANES), tab),
            pl.BlockSpec((tile, LANES), tab),
        ],
        out_specs=[
            pl.BlockSpec((tile, QK_COLS), row),
            pl.BlockSpec((tile, QK_COLS), row),
            pl.BlockSpec((1, ATTN_WIDTH, tile), lambda i: (i // n_seq, 0, i % n_seq)),
            pl.BlockSpec((tile, CONV_WIDTH), row),
            pl.BlockSpec((tile, 2 * D_MODEL), row),
        ],
        out_shape=[
            jax.ShapeDtypeStruct((rows, QK_COLS), BF16),
            jax.ShapeDtypeStruct((rows, QK_COLS), BF16),
            jax.ShapeDtypeStruct((nb, ATTN_WIDTH, seq), BF16),
            jax.ShapeDtypeStruct((rows, CONV_WIDTH), F32),
            jax.ShapeDtypeStruct((rows, 2 * D_MODEL), F32),
        ],
        compiler_params=_params(("arbitrary",)),
        name="in_proj",
    )(x2, ln_g, ln_b, w_bf, bias, cos_t, sa_t, sb_t)


def _rope_tables(length):
    pos = jnp.arange(length, dtype=F32)
    inv = ROPE_THETA ** (-jnp.arange(0, ROPE_DIMS, 2, dtype=F32) / ROPE_DIMS)
    ang = pos[:, None] * inv[None, :]
    cos, sin = jnp.cos(ang), jnp.sin(ang)
    half = ROPE_DIMS // 2
    ones = jnp.ones((length, HEAD_DIM - ROPE_DIMS), F32)
    zeros = jnp.zeros((length, HEAD_DIM - ROPE_DIMS), F32)
    zh = jnp.zeros((length, half), F32)
    cos_h = jnp.concatenate([cos, cos, ones], axis=1)
    sa_h = jnp.concatenate([-sin, zh, zeros], axis=1)
    sb_h = jnp.concatenate([zh, sin, zeros], axis=1)
    rep = LANES // HEAD_DIM
    return (jnp.tile(cos_h, (1, rep)), jnp.tile(sa_h, (1, rep)), jnp.tile(sb_h, (1, rep)))


def _attn_kernel(lq1_ref, lk1_ref, lq2_ref, lk2_ref, sg_ref, q_ref, k_ref, vt_ref, km_ref, vtm_ref,
                 o_ref, m_ref, l_ref, acc_ref):
    qi = pl.program_id(2)
    tq = q_ref.shape[0]
    q = q_ref[...]
    lane = lax.broadcasted_iota(jnp.int32, q.shape, 1)
    zero = jnp.zeros_like(q)
    qm = (jnp.where(lane < HEAD_DIM, q, zero), jnp.where(lane >= HEAD_DIM, q, zero))
    nt = (((1,), (1,)), ((), ()))

    def scores(keys, m):
        return lax.dot_general(keys, qm[m], nt, preferred_element_type=F32)

    km = km_ref[...]
    vtm = vtm_ref[0]
    meta_ok = lax.broadcasted_iota(jnp.int32, (META_PAD, tq), 0) < N_META
    for m in range(2):
        s = jnp.where(meta_ok, scores(km, m), NEG_INF)
        mx = jnp.max(s, axis=0, keepdims=True)
        p = jnp.exp(s - mx)
        m_ref[m] = mx
        l_ref[m] = jnp.sum(p, axis=0, keepdims=True)
        acc_ref[m] = jnp.dot(vtm, p.astype(BF16), preferred_element_type=F32)

    def tile_update(j, mask):
        start = pl.multiple_of(j * tq, tq)
        kt = k_ref[pl.ds(start, tq), :]
        vt = vt_ref[0, :, pl.ds(start, tq)]
        for m in range(2):
            s = scores(kt, m)
            if mask is not None:
                s = jnp.where(mask, s, NEG_INF)
            m_old = m_ref[m]
            m_new = jnp.maximum(m_old, jnp.max(s, axis=0, keepdims=True))
            alpha = jnp.exp(m_old - m_new)
            p = jnp.exp(s - m_new)
            m_ref[m] = m_new
            l_ref[m] = alpha * l_ref[m] + jnp.sum(p, axis=0, keepdims=True)
            acc_ref[m] = alpha * acc_ref[m] + jnp.dot(vt, p.astype(BF16), preferred_element_type=F32)

    def body(j, carry):
        tile_update(j, None)
        return carry

    lax.fori_loop(0, qi, body, 0)
    kc = lax.broadcasted_iota(jnp.int32, (tq, tq), 0) // CHUNK
    qc = lax.broadcasted_iota(jnp.int32, (tq, tq), 1) // CHUNK
    tile_update(qi, kc <= qc)

    lam = (jnp.exp(jnp.sum(lq1_ref[...] * lk1_ref[...], axis=1, keepdims=True))
           - jnp.exp(jnp.sum(lq2_ref[...] * lk2_ref[...], axis=1, keepdims=True)) + LAM_INIT)
    ot = acc_ref[0] / l_ref[0] - lam * (acc_ref[1] / l_ref[1])
    ot = ot * lax.rsqrt(jnp.mean(ot * ot, axis=0, keepdims=True) + LN_EPS)
    o_ref[...] = (ot.T * sg_ref[...] * (1.0 - LAM_INIT)).astype(BF16)


def _attention(nb, seq, q, k, vt, k_meta, vt_meta, lq1, lk1, lq2, lk2, subln_g):
    tq = ATTN_TILE
    nq = seq // tq
    vec = lambda b, h, i: (0, 0)
    return pl.pallas_call(
        _attn_kernel,
        grid=(nb, N_HEADS, nq),
        in_specs=[
            pl.BlockSpec((1, HEAD_DIM), vec),
            pl.BlockSpec((1, HEAD_DIM), vec),
            pl.BlockSpec((1, HEAD_DIM), vec),
            pl.BlockSpec((1, HEAD_DIM), vec),
            pl.BlockSpec((1, V_HEAD_DIM), vec),
            pl.BlockSpec((tq, LANES), lambda b, h, i: (b * nq + i, h)),
            pl.BlockSpec((seq, LANES), lambda b, h, i: (b, h)),
            pl.BlockSpec((1, V_HEAD_DIM, seq), lambda b, h, i: (b, h, 0)),
            pl.BlockSpec((META_PAD, LANES), lambda b, h, i: (0, h)),
            pl.BlockSpec((1, V_HEAD_DIM, META_PAD), lambda b, h, i: (0, h, 0)),
        ],
        out_specs=pl.BlockSpec((tq, V_HEAD_DIM), lambda b, h, i: (b * nq + i, h)),
        out_shape=jax.ShapeDtypeStruct((nb * seq, ATTN_WIDTH), BF16),
        scratch_shapes=[
            pltpu.VMEM((2, 1, tq), F32),
            pltpu.VMEM((2, 1, tq), F32),
            pltpu.VMEM((2, V_HEAD_DIM, tq), F32),
        ],
        compiler_params=_params(("arbitrary", "arbitrary", "arbitrary")),
        name="attn",
    )(lq1, lk1, lq2, lk2, subln_g, q, k, vt, k_meta, vt_meta)


def _conv_kernel(u_ref, prev_ref, um_ref, w_ref, bias_ref, g_ref, b_ref, cu_ref, pad_ref):
    t = pl.program_id(1)
    tc = u_ref.shape[0]
    pad_ref[CONV_HALO:CONV_HALO + tc, :] = u_ref[...]

    @pl.when(t == 0)
    def _():
        pad_ref[0:CONV_HALO - N_META, :] = jnp.zeros((CONV_HALO - N_META, CONV_WIDTH), F32)
        pad_ref[CONV_HALO - N_META:CONV_HALO, :] = um_ref[0:N_META, :]

    @pl.when(t > 0)
    def _():
        pad_ref[0:CONV_HALO, :] = prev_ref[...]

    first = CONV_HALO - (CONV_K - 1)
    acc = jnp.zeros((tc, CONV_WIDTH), F32) + bias_ref[...]
    for kk in range(CONV_K):
        acc = acc + pad_ref[first + kk:first + kk + tc, :] * w_ref[kk:kk + 1, :]
    y = _layer_norm(acc, g_ref[...], b_ref[...])
    cu_ref[...] = (y * jax.nn.sigmoid(y)).astype(BF16)


def _conv(nb, seq, u, u_meta, dw_kernel, dw_bias, ln_g, ln_b):
    tc = CONV_TILE
    nt = seq // tc
    per = tc // CONV_HALO
    const = lambda b, t: (0, 0)
    return pl.pallas_call(
        _conv_kernel,
        grid=(nb, nt),
        in_specs=[
            pl.BlockSpec((tc, CONV_WIDTH), lambda b, t: (b * nt + t, 0)),
            pl.BlockSpec((CONV_HALO, CONV_WIDTH),
                         lambda b, t: (jnp.maximum((b * nt + t) * per - 1, 0), 0)),
            pl.BlockSpec((CONV_HALO, CONV_WIDTH), const),
            pl.BlockSpec((CONV_K + 1, CONV_WIDTH), const),
            pl.BlockSpec((1, CONV_WIDTH), const),
            pl.BlockSpec((1, CONV_WIDTH), const),
            pl.BlockSpec((1, CONV_WIDTH), const),
        ],
        out_specs=pl.BlockSpec((tc, CONV_WIDTH), lambda b, t: (b * nt + t, 0)),
        out_shape=jax.ShapeDtypeStruct((nb * seq, CONV_WIDTH), BF16),
        scratch_shapes=[pltpu.VMEM((CONV_HALO + tc, CONV_WIDTH), F32)],
        compiler_params=_params(("arbitrary", "arbitrary")),
        name="conv",
    )(u, u, u_meta, dw_kernel, dw_bias, ln_g, ln_b)


def _merge_kernel(x_ref, o_ref, cu_ref, sg_ref, lg_ref, lb_ref, wa_ref, wc_ref, bc_ref,
                  wo_ref, bo_ref, g1_ref, b1_ref, h1_ref):
    h = _layer_norm(x_ref[...], lg_ref[...], lb_ref[...])
    attn_d = jnp.dot(o_ref[...], wa_ref[...], preferred_element_type=F32)
    conv_d = jnp.dot(cu_ref[...], wc_ref[...], preferred_element_type=F32) + bc_ref[...]
    z = sg_ref[:, 0:D_MODEL] * attn_d + sg_ref[:, D_MODEL:2 * D_MODEL] * conv_d
    y = jnp.dot(z.astype(BF16), wo_ref[...], preferred_element_type=F32) + bo_ref[...]
    h1_ref[...] = _layer_norm(DEEPNORM_ALPHA * h + y, g1_ref[...], b1_ref[...])


def _merge(rows, x2, o, cu, sg, ln_g, ln_b, wa, wc, bc, wo, bo, g1, b1):
    tile = ROW_TILE
    row = lambda i: (i, 0)
    const = lambda i: (0, 0)
    vec = pl.BlockSpec((1, D_MODEL), const)
    mat = pl.BlockSpec((D_MODEL, D_MODEL), const)
    return pl.pallas_call(
        _merge_kernel,
        grid=(rows // tile,),
        in_specs=[
            pl.BlockSpec((tile, D_MODEL), row),
            pl.BlockSpec((tile, ATTN_WIDTH), row),
            pl.BlockSpec((tile, CONV_WIDTH), row),
            pl.BlockSpec((tile, 2 * D_MODEL), row),
            vec, vec, mat, mat, vec, mat, vec, vec, vec,
        ],
        out_specs=pl.BlockSpec((tile, D_MODEL), row),
        out_shape=jax.ShapeDtypeStruct((rows, D_MODEL), F32),
        compiler_params=_params(("arbitrary",)),
        name="merge",
    )(x2, o, cu, sg, ln_g, ln_b, wa, wc, bc, wo, bo, g1, b1)


def _candidate_pairs():
    return [(a, b) for a in range(PEER_TOPK) for b in range(PEER_TOPK // (a + 1))]


def _top_values(work, count):
    vals = []
    for _ in range(count):
        mx = jnp.max(work, axis=0, keepdims=True)
        vals.append(mx)
        work = jnp.where(work == mx, -jnp.inf, work)
    return vals


def _route_kernel(h_ref, wqt_ref, sk_ref, s_ref, st_ref):
    hb = h_ref[...].astype(BF16)
    nt = (((1,), (1,)), ((), ()))
    qt = lax.dot_general(wqt_ref[...], hb, nt, preferred_element_type=F32).astype(BF16)
    half = PEER_DKEY // 2
    tok = hb.shape[0]
    for hh in range(PEER_HEADS):
        tops = []
        for p in range(2):
            hp = hh * 2 + p
            s = jnp.dot(sk_ref[hp], qt[hp * half:(hp + 1) * half, :], preferred_element_type=F32)
            s_ref[hp] = s
            tops.append(_top_values(s, PEER_TOPK))
        cands = [tops[0][a] + tops[1][b] for a, b in _candidate_pairs()]
        pad = (-len(cands)) % 8
        cand = jnp.concatenate(cands + [jnp.full((pad, tok), -jnp.inf, F32)], axis=0)
        best = _top_values(cand, PEER_TOPK)
        z = jnp.ones_like(best[0])
        for r in range(1, PEER_TOPK):
            z = z + jnp.exp(best[r] - best[0])
        st_ref[hh] = jnp.concatenate(
            [best[PEER_TOPK - 1], tops[0][0], tops[1][0], 1.0 / z,
             jnp.zeros((4, tok), F32)], axis=0)


def _route(rows, h1, wqt, sk):
    tile = ROUTE_TILE
    return pl.pallas_call(
        _route_kernel,
        grid=(rows // tile,),
        in_specs=[
            pl.BlockSpec((tile, D_MODEL), lambda i: (i, 0)),
            pl.BlockSpec((PEER_HEADS * PEER_DKEY, D_MODEL), lambda i: (0, 0)),
            pl.BlockSpec((2 * PEER_HEADS, PEER_KEYS, PEER_DKEY // 2), lambda i: (0, 0, 0)),
        ],
        out_specs=[
            pl.BlockSpec((2 * PEER_HEADS, PEER_KEYS, tile), lambda i: (0, 0, i)),
            pl.BlockSpec((PEER_HEADS, 8, tile), lambda i: (0, 0, i)),
        ],
        out_shape=[
            jax.ShapeDtypeStruct((2 * PEER_HEADS, PEER_KEYS, rows), F32),
            jax.ShapeDtypeStruct((PEER_HEADS, 8, rows), F32),
        ],
        compiler_params=_params(("arbitrary",)),
        name="route",
    )(h1, wqt, sk)


def _peer_kernel(h_ref, u_ref, vt_ref, s_ref, st_ref, g2_ref, b2_ref, out_ref,
                 hb_ref, acc_ref, e1_ref, e2_ref, coef_ref):
    e = pl.program_id(1)
    ne = pl.num_programs(1)
    tok = h_ref.shape[0]

    @pl.when(e == 0)
    def _():
        hb_ref[...] = h_ref[...].astype(BF16)
        acc_ref[...] = jnp.zeros_like(acc_ref)
        for hh in range(PEER_HEADS):
            st = st_ref[hh]
            e1_ref[hh] = jnp.exp(s_ref[2 * hh] - st[1:2, :]) * st[3:4, :]
            e2_ref[hh] = jnp.exp(s_ref[2 * hh + 1] - st[2:3, :])

    nt = (((1,), (1,)), ((), ()))
    act = lax.dot_general(u_ref[...], hb_ref[...], nt, preferred_element_type=F32)
    act = 0.5 * act * (1.0 + lax.erf(act * (2.0 ** -0.5)))
    n_first = PEER_EXP // PEER_KEYS
    row0 = pl.multiple_of(e * n_first, n_first)
    s1 = [s_ref[2 * hh, pl.ds(row0, n_first), :] for hh in range(PEER_HEADS)]
    w1 = [e1_ref[hh, pl.ds(row0, n_first), :] for hh in range(PEER_HEADS)]
    for ii in range(n_first):
        gate = jnp.zeros((PEER_KEYS, tok), F32)
        for hh in range(PEER_HEADS):
            total = s1[hh][ii:ii + 1, :] + s_ref[2 * hh + 1]
            sel = total >= st_ref[hh, 0:1, :]
            gate = gate + jnp.where(sel, e2_ref[hh] * w1[hh][ii:ii + 1, :], 0.0)
        coef_ref[ii * PEER_KEYS:(ii + 1) * PEER_KEYS, :] = (
            gate * act[ii * PEER_KEYS:(ii + 1) * PEER_KEYS, :]).astype(BF16)
    acc_ref[...] += jnp.dot(vt_ref[...], coef_ref[...], preferred_element_type=F32)

    @pl.when(e == ne - 1)
    def _():
        y = DEEPNORM_ALPHA * h_ref[...] + acc_ref[...].T
        out_ref[...] = _layer_norm(y, g2_ref[...], b2_ref[...])


def _peer(rows, h1, u_bf, vt_bf, s, st, g2, b2):
    tok = PEER_TOK
    return pl.pallas_call(
        _peer_kernel,
        grid=(rows // tok, PEER_N // PEER_EXP),
        in_specs=[
            pl.BlockSpec((tok, D_MODEL), lambda t, e: (t, 0)),
            pl.BlockSpec((PEER_EXP, D_MODEL), lambda t, e: (e, 0)),
            pl.BlockSpec((D_MODEL, PEER_EXP), lambda t, e: (0, e)),
            pl.BlockSpec((2 * PEER_HEADS, PEER_KEYS, tok), lambda t, e: (0, 0, t)),
            pl.BlockSpec((PEER_HEADS, 8, tok), lambda t, e: (0, 0, t)),
            pl.BlockSpec((1, D_MODEL), lambda t, e: (0, 0)),
            pl.BlockSpec((1, D_MODEL), lambda t, e: (0, 0)),
        ],
        out_specs=pl.BlockSpec((tok, D_MODEL), lambda t, e: (t, 0)),
        out_shape=jax.ShapeDtypeStruct((rows, D_MODEL), F32),
        scratch_shapes=[
            pltpu.VMEM((tok, D_MODEL), BF16),
            pltpu.VMEM((D_MODEL, tok), F32),
            pltpu.VMEM((PEER_HEADS, PEER_KEYS, tok), F32),
            pltpu.VMEM((PEER_HEADS, PEER_KEYS, tok), F32),
            pltpu.VMEM((PEER_EXP, tok), BF16),
        ],
        compiler_params=_params(("arbitrary", "arbitrary")),
        name="peer",
    )(h1, u_bf, vt_bf, s, st, g2, b2)


def kernel(x, meta, ln_in_g, ln_in_b, w_in, b_in, lambda_q1, lambda_k1, lambda_q2, lambda_k2,
           subln_g, w_attn_proj, dw_kernel, dw_bias, conv_ln_g, conv_ln_b, w_conv_proj,
           b_conv_proj, w_out, b_out, ln1_g, ln1_b, peer_wq, peer_subkeys, peer_u, peer_v,
           ln2_g, ln2_b):
    nb, seq, _ = x.shape
    rows = nb * seq
    assert w_in.shape[0] == DEPTH and seq % PEER_TOK == 0 and rows % PEER_TOK == 0
    vec = lambda a: a.reshape(1, -1)
    x2 = x.reshape(rows, D_MODEL)
    w_bf = w_in[0].astype(BF16)
    bias = vec(b_in[0])
    g_in, b_ln = vec(ln_in_g), vec(ln_in_b)

    cos_t, sa_t, sb_t = _rope_tables(N_META + seq)
    frame_tabs = [t[N_META:] for t in (cos_t, sa_t, sb_t)]
    meta_tabs = [t[:META_PAD] for t in (cos_t, sa_t, sb_t)]
    meta_rows = jnp.pad(meta, ((0, META_PAD - N_META), (0, 0)))

    q, k, vt, u, sg = _in_proj(rows, seq, ROW_TILE, x2, g_in, b_ln, w_bf, bias, *frame_tabs)
    _, k_meta, vt_meta, u_meta, _ = _in_proj(META_PAD, META_PAD, META_PAD, meta_rows, g_in, b_ln,
                                             w_bf, bias, *meta_tabs)

    o = _attention(nb, seq, q, k, vt, k_meta, vt_meta, vec(lambda_q1[0]), vec(lambda_k1[0]),
                   vec(lambda_q2[0]), vec(lambda_k2[0]), vec(subln_g[0]))
    dw = jnp.pad(dw_kernel[0], ((0, 1), (0, 0)))
    cu = _conv(nb, seq, u, u_meta, dw, vec(dw_bias[0]), vec(conv_ln_g[0]), vec(conv_ln_b[0]))
    h1 = _merge(rows, x2, o, cu, sg, g_in, b_ln, w_attn_proj[0].astype(BF16),
                w_conv_proj[0].astype(BF16), vec(b_conv_proj[0]), w_out[0].astype(BF16),
                vec(b_out[0]), vec(ln1_g[0]), vec(ln1_b[0]))

    wqt = peer_wq[0].T.astype(BF16)
    sk = peer_subkeys[0].reshape(2 * PEER_HEADS, PEER_KEYS, PEER_DKEY // 2).astype(BF16)
    s, st = _route(rows, h1, wqt, sk)
    h2 = _peer(rows, h1, peer_u[0].astype(BF16), peer_v[0].T.astype(BF16), s, st,
               vec(ln2_g[0]), vec(ln2_b[0]))
    return h2.reshape(nb, seq, D_MODEL)
```

```python
import functools
import math

import jax
import jax.numpy as jnp
from jax import lax
from jax.experimental import pallas as pl
from jax.experimental.pallas import tpu as pltpu

D_MODEL = 1024
CHUNK = 64
N_META = 16
N_HEADS = 8
HEAD_DIM = 64
V_HEAD_DIM = 2 * HEAD_DIM
QK_COLS = N_HEADS * 2 * HEAD_DIM
ATTN_WIDTH = N_HEADS * V_HEAD_DIM
ROPE_DIMS = HEAD_DIM // 4
ROPE_THETA = 500000.0
NEG_INF = -1e30
CONV_WIDTH = D_MODEL
CONV_K = 31
PEER_HEADS = 8
PEER_KEYS = 128
PEER_N = PEER_KEYS * PEER_KEYS
PEER_DKEY = 256
PEER_TOPK = 16
LN_EPS = 1e-5
DEPTH = 1
DEEPNORM_ALPHA = (2 * DEPTH) ** 0.25
LAM_INIT = 0.8 - 0.6 * math.exp(-0.3 * 0)

LANES = 128
BF16_ROWS = 16
RANK_OUT = 64.0
META_PAD = 128
VMEM_LIMIT = 56 * 1024 * 1024

ROW_TILE = 256
ATTN_TILE = 256
ATTN_KEYS = 128
CONV_TILE = 256
CONV_HALO = 32
ROUTE_TILE = 256
PEER_TOK = 512
PEER_EXP = 1024

F32 = jnp.float32
BF16 = jnp.bfloat16


def _layer_norm(x, g, b):
    mu = jnp.mean(x, axis=-1, keepdims=True)
    xc = x - mu
    var = jnp.mean(xc * xc, axis=-1, keepdims=True)
    return xc * lax.rsqrt(var + LN_EPS) * g + b


def _params(sem, vmem=VMEM_LIMIT):
    return pltpu.CompilerParams(dimension_semantics=sem, vmem_limit_bytes=vmem)


def _in_proj_kernel(x_ref, g_ref, b_ref, w_ref, bias_ref, cos_ref, sa_ref, sb_ref,
                    q_ref, k_ref, vt_ref, u_ref, sg_ref):
    hb = _layer_norm(x_ref[...], g_ref[...], b_ref[...]).astype(BF16)

    def proj(c0, n):
        return (jnp.dot(hb, w_ref[:, c0:c0 + n], preferred_element_type=F32)
                + bias_ref[:, c0:c0 + n])

    cos, sa, sb = cos_ref[...], sa_ref[...], sb_ref[...]

    def rope_store(t, out_ref, scale):
        for hh in range(QK_COLS // LANES):
            th = t[:, hh * LANES:(hh + 1) * LANES]
            r = (th * cos + pltpu.roll(th, LANES - ROPE_DIMS // 2, 1) * sa
                 + pltpu.roll(th, ROPE_DIMS // 2, 1) * sb)
            out_ref[:, hh * LANES:(hh + 1) * LANES] = (r * scale).astype(BF16)

    rope_store(proj(0, QK_COLS), q_ref, HEAD_DIM ** -0.5)
    rope_store(proj(QK_COLS, QK_COLS), k_ref, 1.0)
    vt_ref[0] = proj(2 * QK_COLS, ATTN_WIDTH).T.astype(BF16)
    c0 = 2 * QK_COLS + ATTN_WIDTH
    a = proj(c0, CONV_WIDTH)
    gt = proj(c0 + CONV_WIDTH, CONV_WIDTH)
    u_ref[...] = a * jax.nn.sigmoid(gt)
    sg_ref[...] = jax.nn.sigmoid(proj(c0 + 2 * CONV_WIDTH, 2 * D_MODEL))


def _in_proj(rows, seq, tile, x2, ln_g, ln_b, w_bf, bias, cos_t, sa_t, sb_t):
    n_seq = seq // tile
    nb = rows // seq
    in_cols = w_bf.shape[1]
    row = lambda i: (i, 0)
    const = lambda i: (0, 0)
    tab = lambda i: (i % n_seq, 0)
    return pl.pallas_call(
        _in_proj_kernel,
        grid=(rows // tile,),
        in_specs=[
            pl.BlockSpec((tile, D_MODEL), row),
            pl.BlockSpec((1, D_MODEL), const),
            pl.BlockSpec((1, D_MODEL), const),
            pl.BlockSpec((D_MODEL, in_cols), const, pipeline_mode=pl.Buffered(1)),
            pl.BlockSpec((1, in_cols), const),
            pl.BlockSpec((tile, LANES), tab),
            pl.BlockSpec((tile, LANES), tab),
            pl.BlockSpec((tile, LANES), tab),
        ],
        out_specs=[
            pl.BlockSpec((tile, QK_COLS), row),
            pl.BlockSpec((tile, QK_COLS), row),
            pl.BlockSpec((1, ATTN_WIDTH, tile), lambda i: (i // n_seq, 0, i % n_seq)),
            pl.BlockSpec((tile, CONV_WIDTH), row),
            pl.BlockSpec((tile, 2 * D_MODEL), row),
        ],
        out_shape=[
            jax.ShapeDtypeStruct((rows, QK_COLS), BF16),
            jax.ShapeDtypeStruct((rows, QK_COLS), BF16),
            jax.ShapeDtypeStruct((nb, ATTN_WIDTH, seq), BF16),
            jax.ShapeDtypeStruct((rows, CONV_WIDTH), F32),
            jax.ShapeDtypeStruct((rows, 2 * D_MODEL), F32),
        ],
        compiler_params=_params(("arbitrary",)),
        name="in_proj",
    )(x2, ln_g, ln_b, w_bf, bias, cos_t, sa_t, sb_t)


def _rope_tables(length):
    pos = jnp.arange(length, dtype=F32)
    inv = ROPE_THETA ** (-jnp.arange(0, ROPE_DIMS, 2, dtype=F32) / ROPE_DIMS)
    ang = pos[:, None] * inv[None, :]
    cos, sin = jnp.cos(ang), jnp.sin(ang)
    half = ROPE_DIMS // 2
    ones = jnp.ones((length, HEAD_DIM - ROPE_DIMS), F32)
    zeros = jnp.zeros((length, HEAD_DIM - ROPE_DIMS), F32)
    zh = jnp.zeros((length, half), F32)
    cos_h = jnp.concatenate([cos, cos, ones], axis=1)
    sa_h = jnp.concatenate([-sin, zh, zeros], axis=1)
    sb_h = jnp.concatenate([zh, sin, zeros], axis=1)
    rep = LANES // HEAD_DIM
    return (jnp.tile(cos_h, (1, rep)), jnp.tile(sa_h, (1, rep)), jnp.tile(sb_h, (1, rep)))


def _attn_kernel(lq1_ref, lk1_ref, lq2_ref, lk2_ref, subg_ref, q_ref, k_ref, vt_ref, km_ref, vtm_ref,
                 o_ref, qm_ref, m_ref, l_ref, acc_ref):
    qi = pl.program_id(1)
    tq = q_ref.shape[0]
    nt = (((1,), (1,)), ((), ()))
    lane = lax.broadcasted_iota(jnp.int32, (tq, LANES), 1)
    maps = [(hh, m) for hh in range(N_HEADS) for m in range(2)]

    for hh in range(N_HEADS):
        qh = q_ref[:, hh * LANES:(hh + 1) * LANES]
        zero = jnp.zeros_like(qh)
        qm_ref[2 * hh] = jnp.where(lane < HEAD_DIM, qh, zero)
        qm_ref[2 * hh + 1] = jnp.where(lane >= HEAD_DIM, qh, zero)

    def scores(keys, c):
        return lax.dot_general(keys, qm_ref[c], nt, preferred_element_type=F32)

    meta_ok = lax.broadcasted_iota(jnp.int32, (META_PAD, tq), 0) < N_META
    for hh, m in maps:
        c = 2 * hh + m
        s = jnp.where(meta_ok, scores(km_ref[:, hh * LANES:(hh + 1) * LANES], c), NEG_INF)
        mx = jnp.max(s, axis=0, keepdims=True)
        p = jnp.exp(s - mx)
        m_ref[c] = mx
        l_ref[c] = jnp.sum(p, axis=0, keepdims=True)
        acc_ref[c] = jnp.dot(vtm_ref[0, hh * LANES:(hh + 1) * LANES, :], p.astype(BF16),
                             preferred_element_type=F32)

    tk = ATTN_KEYS
    per_q = tq // tk

    def tile_update(j, mask):
        start = pl.multiple_of(j * tk, tk)
        for hh, m in maps:
            c = 2 * hh + m
            s = scores(k_ref[pl.ds(start, tk), hh * LANES:(hh + 1) * LANES], c)
            if mask is not None:
                s = jnp.where(mask, s, NEG_INF)
            m_old = m_ref[c]
            m_new = jnp.maximum(m_old, jnp.max(s, axis=0, keepdims=True))
            alpha = jnp.exp(m_old - m_new)
            p = jnp.exp(s - m_new)
            m_ref[c] = m_new
            l_ref[c] = alpha * l_ref[c] + jnp.sum(p, axis=0, keepdims=True)
            acc_ref[c] = alpha * acc_ref[c] + jnp.dot(
                vt_ref[0, hh * LANES:(hh + 1) * LANES, pl.ds(start, tk)], p.astype(BF16),
                preferred_element_type=F32)

    def body(j, carry):
        tile_update(j, None)
        return carry

    lax.fori_loop(0, qi * per_q, body, 0)
    qc = lax.broadcasted_iota(jnp.int32, (tk, tq), 1) // CHUNK
    for d in range(per_q):
        kc = (lax.broadcasted_iota(jnp.int32, (tk, tq), 0) + d * tk) // CHUNK
        tile_update(qi * per_q + d, kc <= qc)

    lam = (jnp.exp(jnp.sum(lq1_ref[...] * lk1_ref[...], axis=1, keepdims=True))
           - jnp.exp(jnp.sum(lq2_ref[...] * lk2_ref[...], axis=1, keepdims=True)) + LAM_INIT)
    for hh in range(N_HEADS):
        c = 2 * hh
        ot = acc_ref[c] / l_ref[c] - lam * (acc_ref[c + 1] / l_ref[c + 1])
        ot = ot * lax.rsqrt(jnp.mean(ot * ot, axis=0, keepdims=True) + LN_EPS)
        o_ref[:, hh * LANES:(hh + 1) * LANES] = (
            ot.T * subg_ref[...] * (1.0 - LAM_INIT)).astype(BF16)


def _attention(nb, seq, q, k, vt, k_meta, vt_meta, lq1, lk1, lq2, lk2, subln_g):
    tq = ATTN_TILE
    nq = seq // tq
    n_maps = 2 * N_HEADS
    vec = lambda b, i: (0, 0)
    return pl.pallas_call(
        _attn_kernel,
        grid=(nb, nq),
        in_specs=[
            pl.BlockSpec((1, HEAD_DIM), vec),
            pl.BlockSpec((1, HEAD_DIM), vec),
            pl.BlockSpec((1, HEAD_DIM), vec),
            pl.BlockSpec((1, HEAD_DIM), vec),
            pl.BlockSpec((1, V_HEAD_DIM), vec),
            pl.BlockSpec((tq, QK_COLS), lambda b, i: (b * nq + i, 0)),
            pl.BlockSpec((seq, QK_COLS), lambda b, i: (b, 0)),
            pl.BlockSpec((1, ATTN_WIDTH, seq), lambda b, i: (b, 0, 0)),
            pl.BlockSpec((META_PAD, QK_COLS), vec),
            pl.BlockSpec((1, ATTN_WIDTH, META_PAD), lambda b, i: (0, 0, 0)),
        ],
        out_specs=pl.BlockSpec((tq, ATTN_WIDTH), lambda b, i: (b * nq + i, 0)),
        out_shape=jax.ShapeDtypeStruct((nb * seq, ATTN_WIDTH), BF16),
        scratch_shapes=[
            pltpu.VMEM((n_maps, tq, LANES), BF16),
            pltpu.VMEM((n_maps, 1, tq), F32),
            pltpu.VMEM((n_maps, 1, tq), F32),
            pltpu.VMEM((n_maps, V_HEAD_DIM, tq), F32),
        ],
        compiler_params=_params(("arbitrary", "arbitrary")),
        name="attn",
    )(lq1, lk1, lq2, lk2, subln_g, q, k, vt, k_meta, vt_meta)


def _conv_kernel(u_ref, prev_ref, um_ref, w_ref, bias_ref, g_ref, b_ref, cu_ref, pad_ref):
    t = pl.program_id(1)
    tc = u_ref.shape[0]
    pad_ref[CONV_HALO:CONV_HALO + tc, :] = u_ref[...]

    @pl.when(t == 0)
    def _():
        pad_ref[0:CONV_HALO - N_META, :] = jnp.zeros((CONV_HALO - N_META, CONV_WIDTH), F32)
        pad_ref[CONV_HALO - N_META:CONV_HALO, :] = um_ref[0:N_META, :]

    @pl.when(t > 0)
    def _():
        pad_ref[0:CONV_HALO, :] = prev_ref[...]

    first = CONV_HALO - (CONV_K - 1)
    acc = jnp.zeros((tc, CONV_WIDTH), F32) + bias_ref[...]
    for kk in range(CONV_K):
        acc = acc + pad_ref[first + kk:first + kk + tc, :] * w_ref[kk:kk + 1, :]
    y = _layer_norm(acc, g_ref[...], b_ref[...])
    cu_ref[...] = (y * jax.nn.sigmoid(y)).astype(BF16)


def _conv(nb, seq, u, u_meta, dw_kernel, dw_bias, ln_g, ln_b):
    tc = CONV_TILE
    nt = seq // tc
    per = tc // CONV_HALO
    const = lambda b, t: (0, 0)
    return pl.pallas_call(
        _conv_kernel,
        grid=(nb, nt),
        in_specs=[
            pl.BlockSpec((tc, CONV_WIDTH), lambda b, t: (b * nt + t, 0)),
            pl.BlockSpec((CONV_HALO, CONV_WIDTH),
                         lambda b, t: (jnp.maximum((b * nt + t) * per - 1, 0), 0)),
            pl.BlockSpec((CONV_HALO, CONV_WIDTH), const),
            pl.BlockSpec((CONV_K + 1, CONV_WIDTH), const),
            pl.BlockSpec((1, CONV_WIDTH), const),
            pl.BlockSpec((1, CONV_WIDTH), const),
            pl.BlockSpec((1, CONV_WIDTH), const),
        ],
        out_specs=pl.BlockSpec((tc, CONV_WIDTH), lambda b, t: (b * nt + t, 0)),
        out_shape=jax.ShapeDtypeStruct((nb * seq, CONV_WIDTH), BF16),
        scratch_shapes=[pltpu.VMEM((CONV_HALO + tc, CONV_WIDTH), F32)],
        compiler_params=_params(("arbitrary", "arbitrary")),
        name="conv",
    )(u, u, u_meta, dw_kernel, dw_bias, ln_g, ln_b)


def _merge_kernel(x_ref, o_ref, cu_ref, sg_ref, lg_ref, lb_ref, wa_ref, wc_ref, bc_ref,
                  wo_ref, bo_ref, g1_ref, b1_ref, h1_ref):
    h = _layer_norm(x_ref[...], lg_ref[...], lb_ref[...])
    attn_d = jnp.dot(o_ref[...], wa_ref[...], preferred_element_type=F32)
    conv_d = jnp.dot(cu_ref[...], wc_ref[...], preferred_element_type=F32) + bc_ref[...]
    z = sg_ref[:, 0:D_MODEL] * attn_d + sg_ref[:, D_MODEL:2 * D_MODEL] * conv_d
    y = jnp.dot(z.astype(BF16), wo_ref[...], preferred_element_type=F32) + bo_ref[...]
    h1_ref[...] = _layer_norm(DEEPNORM_ALPHA * h + y, g1_ref[...], b1_ref[...])


def _merge(rows, x2, o, cu, sg, ln_g, ln_b, wa, wc, bc, wo, bo, g1, b1):
    tile = ROW_TILE
    row = lambda i: (i, 0)
    const = lambda i: (0, 0)
    vec = pl.BlockSpec((1, D_MODEL), const)
    mat = pl.BlockSpec((D_MODEL, D_MODEL), const)
    return pl.pallas_call(
        _merge_kernel,
        grid=(rows // tile,),
        in_specs=[
            pl.BlockSpec((tile, D_MODEL), row),
            pl.BlockSpec((tile, ATTN_WIDTH), row),
            pl.BlockSpec((tile, CONV_WIDTH), row),
            pl.BlockSpec((tile, 2 * D_MODEL), row),
            vec, vec, mat, mat, vec, mat, vec, vec, vec,
        ],
        out_specs=pl.BlockSpec((tile, D_MODEL), row),
        out_shape=jax.ShapeDtypeStruct((rows, D_MODEL), F32),
        compiler_params=_params(("arbitrary",)),
        name="merge",
    )(x2, o, cu, sg, ln_g, ln_b, wa, wc, bc, wo, bo, g1, b1)


def _candidate_pairs():
    return [(a, b) for a in range(PEER_TOPK) for b in range(PEER_TOPK // (a + 1))]


def _top_values(work, count, want_rank):
    vals = []
    rank = jnp.full(work.shape, RANK_OUT, F32) if want_rank else None
    for r in range(count):
        mx = jnp.max(work, axis=0, keepdims=True)
        vals.append(mx)
        hit = work == mx
        if want_rank:
            rank = jnp.where(hit, float(r), rank)
        work = jnp.where(hit, -jnp.inf, work)
    return vals, rank


def _paired_bf16_words(x):
    bits = pltpu.bitcast(x.astype(BF16).astype(F32), jnp.uint32)
    return bits | lax.shift_right_logical(bits, jnp.uint32(16))


def _route_kernel(h_ref, wqt_ref, sk_ref, r2_ref, e2_ref, n1_ref, w1_ref):
    hb = h_ref[...].astype(BF16)
    nt = (((1,), (1,)), ((), ()))
    qt = lax.dot_general(wqt_ref[...], hb, nt, preferred_element_type=F32).astype(BF16)
    half = PEER_DKEY // 2
    tok = hb.shape[0]
    for hh in range(PEER_HEADS):
        s1 = jnp.dot(sk_ref[2 * hh], qt[2 * hh * half:(2 * hh + 1) * half, :],
                     preferred_element_type=F32)
        s2 = jnp.dot(sk_ref[2 * hh + 1], qt[(2 * hh + 1) * half:(2 * hh + 2) * half, :],
                     preferred_element_type=F32)
        top1, rank1 = _top_values(s1, PEER_TOPK, True)
        top2, rank2 = _top_values(s2, PEER_TOPK, True)
        pairs = _candidate_pairs()
        cands = [top1[a] + top2[b] for a, b in pairs]
        pad = (-len(cands)) % 8
        cand = jnp.concatenate(cands + [jnp.full((pad, tok), -jnp.inf, F32)], axis=0)
        best, _ = _top_values(cand, PEER_TOPK, False)
        tau = best[PEER_TOPK - 1]
        z = jnp.ones_like(tau)
        for r in range(1, PEER_TOPK):
            z = z + jnp.exp(best[r] - best[0])
        n1 = jnp.zeros_like(s1)
        for a in range(PEER_TOPK):
            count = jnp.zeros_like(tau)
            for (pa, _), c in zip(pairs, cands):
                if pa == a:
                    count = count + jnp.where(c >= tau, 1.0, 0.0)
            n1 = jnp.where(rank1 == float(a), count, n1)
        r2_ref[hh] = rank2.astype(BF16)
        e2_ref[hh] = jnp.exp(s2 - top2[0]).astype(BF16)
        n1_ref[hh] = _paired_bf16_words(n1)
        w1_ref[hh] = _paired_bf16_words(jnp.exp(s1 - top1[0]) * (1.0 / z))


def _route(rows, h1, wqt, sk):
    tile = ROUTE_TILE
    blk = pl.BlockSpec((PEER_HEADS, PEER_KEYS, tile), lambda i: (0, 0, i))
    shape = (PEER_HEADS, PEER_KEYS, rows)
    return pl.pallas_call(
        _route_kernel,
        grid=(rows // tile,),
        in_specs=[
            pl.BlockSpec((tile, D_MODEL), lambda i: (i, 0)),
            pl.BlockSpec((PEER_HEADS * PEER_DKEY, D_MODEL), lambda i: (0, 0)),
            pl.BlockSpec((2 * PEER_HEADS, PEER_KEYS, PEER_DKEY // 2), lambda i: (0, 0, 0)),
        ],
        out_specs=[blk, blk, blk, blk],
        out_shape=[
            jax.ShapeDtypeStruct(shape, BF16),
            jax.ShapeDtypeStruct(shape, BF16),
            jax.ShapeDtypeStruct(shape, jnp.uint32),
            jax.ShapeDtypeStruct(shape, jnp.uint32),
        ],
        compiler_params=_params(("arbitrary",)),
        name="route",
    )(h1, wqt, sk)


def _peer_kernel(h_ref, u_ref, vt_ref, r2_ref, e2_ref, n1_ref, w1_ref, g2_ref, b2_ref, out_ref,
                 hb_ref, acc_ref, coef_ref):
    e = pl.program_id(1)
    ne = pl.num_programs(1)
    tok = h_ref.shape[0]

    @pl.when(e == 0)
    def _():
        hb_ref[...] = h_ref[...].astype(BF16)
        acc_ref[...] = jnp.zeros_like(acc_ref)

    nt = (((1,), (1,)), ((), ()))
    act = lax.dot_general(u_ref[...], hb_ref[...], nt, preferred_element_type=F32)
    act = (0.5 * act * (1.0 + lax.erf(act * (2.0 ** -0.5)))).astype(BF16)
    n_first = PEER_EXP // PEER_KEYS
    row0 = pl.multiple_of(e * n_first, n_first)
    n_rows = [n1_ref[hh, pl.ds(row0, n_first), :] for hh in range(PEER_HEADS)]
    w_rows = [w1_ref[hh, pl.ds(row0, n_first), :] for hh in range(PEER_HEADS)]

    def packed_row(rows, ii):
        word_tile = jnp.broadcast_to(rows[ii:ii + 1, :], (n_first, tok))
        return pltpu.bitcast(word_tile, BF16)

    for ii in range(n_first):
        n_b = [packed_row(n_rows[hh], ii) for hh in range(PEER_HEADS)]
        w_b = [packed_row(w_rows[hh], ii) for hh in range(PEER_HEADS)]
        for c in range(PEER_KEYS // BF16_ROWS):
            js = slice(c * BF16_ROWS, (c + 1) * BF16_ROWS)
            gate = jnp.zeros((BF16_ROWS, tok), BF16)
            for hh in range(PEER_HEADS):
                e2 = e2_ref[hh, js, :]
                gate = gate + jnp.where(r2_ref[hh, js, :] < n_b[hh], e2 * w_b[hh], jnp.zeros_like(e2))
            r = ii * PEER_KEYS + c * BF16_ROWS
            coef_ref[r:r + BF16_ROWS, :] = gate * act[r:r + BF16_ROWS, :]
    acc_ref[...] += jnp.dot(vt_ref[...], coef_ref[...], preferred_element_type=F32)

    @pl.when(e == ne - 1)
    def _():
        y = DEEPNORM_ALPHA * h_ref[...] + acc_ref[...].T
        out_ref[...] = _layer_norm(y, g2_ref[...], b2_ref[...])


def _peer(rows, h1, u_bf, vt_bf, r2, e2, n1, w1, g2, b2):
    tok = PEER_TOK
    sel = pl.BlockSpec((PEER_HEADS, PEER_KEYS, tok), lambda t, e: (0, 0, t))
    return pl.pallas_call(
        _peer_kernel,
        grid=(rows // tok, PEER_N // PEER_EXP),
        in_specs=[
            pl.BlockSpec((tok, D_MODEL), lambda t, e: (t, 0)),
            pl.BlockSpec((PEER_EXP, D_MODEL), lambda t, e: (e, 0)),
            pl.BlockSpec((D_MODEL, PEER_EXP), lambda t, e: (0, e)),
            sel, sel, sel, sel,
            pl.BlockSpec((1, D_MODEL), lambda t, e: (0, 0)),
            pl.BlockSpec((1, D_MODEL), lambda t, e: (0, 0)),
        ],
        out_specs=pl.BlockSpec((tok, D_MODEL), lambda t, e: (t, 0)),
        out_shape=jax.ShapeDtypeStruct((rows, D_MODEL), F32),
        scratch_shapes=[
            pltpu.VMEM((tok, D_MODEL), BF16),
            pltpu.VMEM((D_MODEL, tok), F32),
            pltpu.VMEM((PEER_EXP, tok), BF16),
        ],
        compiler_params=_params(("arbitrary", "arbitrary")),
        name="peer",
    )(h1, u_bf, vt_bf, r2, e2, n1, w1, g2, b2)


def kernel(x, meta, ln_in_g, ln_in_b, w_in, b_in, lambda_q1, lambda_k1, lambda_q2, lambda_k2,
           subln_g, w_attn_proj, dw_kernel, dw_bias, conv_ln_g, conv_ln_b, w_conv_proj,
           b_conv_proj, w_out, b_out, ln1_g, ln1_b, peer_wq, peer_subkeys, peer_u, peer_v,
           ln2_g, ln2_b):
    nb, seq, _ = x.shape
    rows = nb * seq
    assert w_in.shape[0] == DEPTH and seq % PEER_TOK == 0 and rows % PEER_TOK == 0
    vec = lambda a: a.reshape(1, -1)
    x2 = x.reshape(rows, D_MODEL)
    w_bf = w_in[0].astype(BF16)
    bias = vec(b_in[0])
    g_in, b_ln = vec(ln_in_g), vec(ln_in_b)

    cos_t, sa_t, sb_t = _rope_tables(N_META + seq)
    frame_tabs = [t[N_META:] for t in (cos_t, sa_t, sb_t)]
    meta_tabs = [t[:META_PAD] for t in (cos_t, sa_t, sb_t)]
    meta_rows = jnp.pad(meta, ((0, META_PAD - N_META), (0, 0)))

    q, k, vt, u, sg = _in_proj(rows, seq, ROW_TILE, x2, g_in, b_ln, w_bf, bias, *frame_tabs)
    _, k_meta, vt_meta, u_meta, _ = _in_proj(META_PAD, META_PAD, META_PAD, meta_rows, g_in, b_ln,
                                             w_bf, bias, *meta_tabs)

    o = _attention(nb, seq, q, k, vt, k_meta, vt_meta, vec(lambda_q1[0]), vec(lambda_k1[0]),
                   vec(lambda_q2[0]), vec(lambda_k2[0]), vec(subln_g[0]))
    dw = jnp.pad(dw_kernel[0], ((0, 1), (0, 0)))
    cu = _conv(nb, seq, u, u_meta, dw, vec(dw_bias[0]), vec(conv_ln_g[0]), vec(conv_ln_b[0]))
    h1 = _merge(rows, x2, o, cu, sg, g_in, b_ln, w_attn_proj[0].astype(BF16),
                w_conv_proj[0].astype(BF16), vec(b_conv_proj[0]), w_out[0].astype(BF16),
                vec(b_out[0]), vec(ln1_g[0]), vec(ln1_b[0]))

    wqt = peer_wq[0].T.astype(BF16)
    sk = peer_subkeys[0].reshape(2 * PEER_HEADS, PEER_KEYS, PEER_DKEY // 2).astype(BF16)
    r2, e2, n1, w1 = _route(rows, h1, wqt, sk)
    h2 = _peer(rows, h1, peer_u[0].astype(BF16), peer_v[0].T.astype(BF16), r2, e2, n1, w1,
               vec(ln2_g[0]), vec(ln2_b[0]))
    return h2.reshape(nb, seq, D_MODEL)
```

```python
import functools
import math

import jax
import jax.numpy as jnp
from jax import lax
from jax.experimental import pallas as pl
from jax.experimental.pallas import tpu as pltpu

D_MODEL = 1024
CHUNK = 64
N_META = 16
N_HEADS = 8
HEAD_DIM = 64
V_HEAD_DIM = 2 * HEAD_DIM
QK_COLS = N_HEADS * 2 * HEAD_DIM
ATTN_WIDTH = N_HEADS * V_HEAD_DIM
ROPE_DIMS = HEAD_DIM // 4
ROPE_THETA = 500000.0
NEG_INF = -1e30
CONV_WIDTH = D_MODEL
CONV_K = 31
PEER_HEADS = 8
PEER_KEYS = 128
PEER_N = PEER_KEYS * PEER_KEYS
PEER_DKEY = 256
PEER_TOPK = 16
LN_EPS = 1e-5
DEPTH = 1
DEEPNORM_ALPHA = (2 * DEPTH) ** 0.25
LAM_INIT = 0.8 - 0.6 * math.exp(-0.3 * 0)
LOG2_E = math.log2(math.e)

LANES = 128
SUBLANES = 8
CONV_ROWS = 128
BF16_ROWS = 16
CAND_ROWS = 64
META_PAD = 128
VMEM_LIMIT = 56 * 1024 * 1024

ROW_TILE = 256
ATTN_TILE = 256
ATTN_KEYS = 128
CONV_TILE = 256
CONV_HALO = 32
ROUTE_TILE = 256
PEER_TOK = 512
PEER_EXP = 1024

F32 = jnp.float32
BF16 = jnp.bfloat16


def _layer_norm(x, g, b):
    mu = jnp.mean(x, axis=-1, keepdims=True)
    xc = x - mu
    var = jnp.mean(xc * xc, axis=-1, keepdims=True)
    return xc * lax.rsqrt(var + LN_EPS) * g + b


def _params(sem, vmem=VMEM_LIMIT):
    return pltpu.CompilerParams(dimension_semantics=sem, vmem_limit_bytes=vmem)


def _in_proj_kernel(x_ref, g_ref, b_ref, w_ref, bias_ref, cos_ref, sa_ref, sb_ref,
                    q_ref, k_ref, vt_ref, u_ref, sg_ref):
    hb = _layer_norm(x_ref[...], g_ref[...], b_ref[...]).astype(BF16)

    def proj(c0, n):
        return (jnp.dot(hb, w_ref[:, c0:c0 + n], preferred_element_type=F32)
                + bias_ref[:, c0:c0 + n])

    cos, sa, sb = cos_ref[...], sa_ref[...], sb_ref[...]

    def rope_store(t, out_ref, scale):
        for hh in range(QK_COLS // LANES):
            th = t[:, hh * LANES:(hh + 1) * LANES]
            r = (th * cos + pltpu.roll(th, LANES - ROPE_DIMS // 2, 1) * sa
                 + pltpu.roll(th, ROPE_DIMS // 2, 1) * sb)
            out_ref[:, hh * LANES:(hh + 1) * LANES] = (r * scale).astype(BF16)

    rope_store(proj(0, QK_COLS), q_ref, HEAD_DIM ** -0.5 * LOG2_E)
    rope_store(proj(QK_COLS, QK_COLS), k_ref, 1.0)
    vt_ref[0] = proj(2 * QK_COLS, ATTN_WIDTH).T.astype(BF16)
    c0 = 2 * QK_COLS + ATTN_WIDTH
    a = proj(c0, CONV_WIDTH)
    gt = proj(c0 + CONV_WIDTH, CONV_WIDTH)
    u_ref[...] = a * jax.nn.sigmoid(gt)
    sg_ref[...] = jax.nn.sigmoid(proj(c0 + 2 * CONV_WIDTH, 2 * D_MODEL))


def _in_proj(rows, seq, tile, x2, ln_g, ln_b, w_bf, bias, cos_t, sa_t, sb_t):
    n_seq = seq // tile
    nb = rows // seq
    in_cols = w_bf.shape[1]
    row = lambda i: (i, 0)
    const = lambda i: (0, 0)
    tab = lambda i: (i % n_seq, 0)
    return pl.pallas_call(
        _in_proj_kernel,
        grid=(rows // tile,),
        in_specs=[
            pl.BlockSpec((tile, D_MODEL), row),
            pl.BlockSpec((1, D_MODEL), const),
            pl.BlockSpec((1, D_MODEL), const),
            pl.BlockSpec((D_MODEL, in_cols), const, pipeline_mode=pl.Buffered(1)),
            pl.BlockSpec((1, in_cols), const),
            pl.BlockSpec((tile, LANES), tab),
            pl.BlockSpec((tile, LANES), tab),
            pl.BlockSpec((tile, LANES), tab),
        ],
        out_specs=[
            pl.BlockSpec((tile, QK_COLS), row),
            pl.BlockSpec((tile, QK_COLS), row),
            pl.BlockSpec((1, ATTN_WIDTH, tile), lambda i: (i // n_seq, 0, i % n_seq)),
            pl.BlockSpec((tile, CONV_WIDTH), row),
            pl.BlockSpec((tile, 2 * D_MODEL), row),
        ],
        out_shape=[
            jax.ShapeDtypeStruct((rows, QK_COLS), BF16),
            jax.ShapeDtypeStruct((rows, QK_COLS), BF16),
            jax.ShapeDtypeStruct((nb, ATTN_WIDTH, seq), BF16),
            jax.ShapeDtypeStruct((rows, CONV_WIDTH), F32),
            jax.ShapeDtypeStruct((rows, 2 * D_MODEL), F32),
        ],
        compiler_params=_params(("arbitrary",)),
        name="in_proj",
    )(x2, ln_g, ln_b, w_bf, bias, cos_t, sa_t, sb_t)


def _rope_tables(length):
    pos = jnp.arange(length, dtype=F32)
    inv = ROPE_THETA ** (-jnp.arange(0, ROPE_DIMS, 2, dtype=F32) / ROPE_DIMS)
    ang = pos[:, None] * inv[None, :]
    cos, sin = jnp.cos(ang), jnp.sin(ang)
    half = ROPE_DIMS // 2
    ones = jnp.ones((length, HEAD_DIM - ROPE_DIMS), F32)
    zeros = jnp.zeros((length, HEAD_DIM - ROPE_DIMS), F32)
    zh = jnp.zeros((length, half), F32)
    cos_h = jnp.concatenate([cos, cos, ones], axis=1)
    sa_h = jnp.concatenate([-sin, zh, zeros], axis=1)
    sb_h = jnp.concatenate([zh, sin, zeros], axis=1)
    rep = LANES // HEAD_DIM
    return (jnp.tile(cos_h, (1, rep)), jnp.tile(sa_h, (1, rep)), jnp.tile(sb_h, (1, rep)))


def _attn_kernel(lq1_ref, lk1_ref, lq2_ref, lk2_ref, subg_ref, q_ref, k_ref, vt_ref, km_ref, vtm_ref,
                 o_ref, qm_ref, m_ref, l_ref, acc_ref):
    qi = pl.program_id(1)
    tq = q_ref.shape[0]
    nt = (((1,), (1,)), ((), ()))
    lane = lax.broadcasted_iota(jnp.int32, (tq, LANES), 1)
    maps = [(hh, m) for hh in range(N_HEADS) for m in range(2)]

    for hh in range(N_HEADS):
        qh = q_ref[:, hh * LANES:(hh + 1) * LANES]
        zero = jnp.zeros_like(qh)
        qm_ref[2 * hh] = jnp.where(lane < HEAD_DIM, qh, zero)
        qm_ref[2 * hh + 1] = jnp.where(lane >= HEAD_DIM, qh, zero)

    def scores(keys, c):
        return lax.dot_general(keys, qm_ref[c], nt, preferred_element_type=F32)

    meta_ok = lax.broadcasted_iota(jnp.int32, (META_PAD, tq), 0) < N_META
    for hh, m in maps:
        c = 2 * hh + m
        s = jnp.where(meta_ok, scores(km_ref[:, hh * LANES:(hh + 1) * LANES], c), NEG_INF)
        mx = jnp.max(s, axis=0, keepdims=True)
        p = jnp.exp2(s - mx)
        m_ref[c] = mx
        l_ref[c] = jnp.sum(p, axis=0, keepdims=True)
        acc_ref[c] = jnp.dot(vtm_ref[0, hh * LANES:(hh + 1) * LANES, :], p.astype(BF16),
                             preferred_element_type=F32)

    tk = ATTN_KEYS
    per_q = tq // tk

    def tile_update(j, mask):
        start = pl.multiple_of(j * tk, tk)
        for hh, m in maps:
            c = 2 * hh + m
            s = scores(k_ref[pl.ds(start, tk), hh * LANES:(hh + 1) * LANES], c)
            if mask is not None:
                s = jnp.where(mask, s, NEG_INF)
            m_old = m_ref[c]
            m_new = jnp.maximum(m_old, jnp.max(s, axis=0, keepdims=True))
            alpha = jnp.exp2(m_old - m_new)
            p = jnp.exp2(s - m_new)
            m_ref[c] = m_new
            l_ref[c] = alpha * l_ref[c] + jnp.sum(p, axis=0, keepdims=True)
            acc_ref[c] = alpha * acc_ref[c] + jnp.dot(
                vt_ref[0, hh * LANES:(hh + 1) * LANES, pl.ds(start, tk)], p.astype(BF16),
                preferred_element_type=F32)

    def body(j, carry):
        tile_update(j, None)
        return carry

    lax.fori_loop(0, qi * per_q, body, 0)
    qc = lax.broadcasted_iota(jnp.int32, (tk, tq), 1) // CHUNK
    for d in range(per_q):
        kc = (lax.broadcasted_iota(jnp.int32, (tk, tq), 0) + d * tk) // CHUNK
        tile_update(qi * per_q + d, kc <= qc)

    lam = (jnp.exp(jnp.sum(lq1_ref[...] * lk1_ref[...], axis=1, keepdims=True))
           - jnp.exp(jnp.sum(lq2_ref[...] * lk2_ref[...], axis=1, keepdims=True)) + LAM_INIT)
    for hh in range(N_HEADS):
        c = 2 * hh
        ot = acc_ref[c] / l_ref[c] - lam * (acc_ref[c + 1] / l_ref[c + 1])
        ot = ot * lax.rsqrt(jnp.mean(ot * ot, axis=0, keepdims=True) + LN_EPS)
        o_ref[:, hh * LANES:(hh + 1) * LANES] = (
            ot.T * subg_ref[...] * (1.0 - LAM_INIT)).astype(BF16)


def _attention(nb, seq, q, k, vt, k_meta, vt_meta, lq1, lk1, lq2, lk2, subln_g):
    tq = ATTN_TILE
    nq = seq // tq
    n_maps = 2 * N_HEADS
    vec = lambda b, i: (0, 0)
    return pl.pallas_call(
        _attn_kernel,
        grid=(nb, nq),
        in_specs=[
            pl.BlockSpec((1, HEAD_DIM), vec),
            pl.BlockSpec((1, HEAD_DIM), vec),
            pl.BlockSpec((1, HEAD_DIM), vec),
            pl.BlockSpec((1, HEAD_DIM), vec),
            pl.BlockSpec((1, V_HEAD_DIM), vec),
            pl.BlockSpec((tq, QK_COLS), lambda b, i: (b * nq + i, 0)),
            pl.BlockSpec((seq, QK_COLS), lambda b, i: (b, 0)),
            pl.BlockSpec((1, ATTN_WIDTH, seq), lambda b, i: (b, 0, 0)),
            pl.BlockSpec((META_PAD, QK_COLS), vec),
            pl.BlockSpec((1, ATTN_WIDTH, META_PAD), lambda b, i: (0, 0, 0)),
        ],
        out_specs=pl.BlockSpec((tq, ATTN_WIDTH), lambda b, i: (b * nq + i, 0)),
        out_shape=jax.ShapeDtypeStruct((nb * seq, ATTN_WIDTH), BF16),
        scratch_shapes=[
            pltpu.VMEM((n_maps, tq, LANES), BF16),
            pltpu.VMEM((n_maps, 1, tq), F32),
            pltpu.VMEM((n_maps, 1, tq), F32),
            pltpu.VMEM((n_maps, V_HEAD_DIM, tq), F32),
        ],
        compiler_params=_params(("arbitrary", "arbitrary")),
        name="attn",
    )(lq1, lk1, lq2, lk2, subln_g, q, k, vt, k_meta, vt_meta)


def _conv_kernel(u_ref, prev_ref, um_ref, w_ref, bias_ref, g_ref, b_ref, cu_ref, pad_ref, sh_ref):
    t = pl.program_id(1)
    tc = u_ref.shape[0]
    pad_ref[CONV_HALO:CONV_HALO + tc, :] = u_ref[...]

    @pl.when(t == 0)
    def _():
        pad_ref[0:CONV_HALO - N_META, :] = jnp.zeros((CONV_HALO - N_META, CONV_WIDTH), F32)
        pad_ref[CONV_HALO - N_META:CONV_HALO, :] = um_ref[0:N_META, :]

    @pl.when(t > 0)
    def _():
        pad_ref[0:CONV_HALO, :] = prev_ref[...]

    first = CONV_HALO - (CONV_K - 1)
    span = tc + (first + CONV_K - 1) // SUBLANES * SUBLANES - SUBLANES
    for o in range(1, SUBLANES):
        sh_ref[o - 1] = pad_ref[o:o + span, :]

    for r0 in range(0, tc, CONV_ROWS):
        acc = jnp.zeros((CONV_ROWS, CONV_WIDTH), F32) + bias_ref[...]
        for kk in range(CONV_K):
            o = (first + kk) % SUBLANES
            base = r0 + first + kk - o
            src = pad_ref if o == 0 else sh_ref.at[o - 1]
            acc = acc + src[base:base + CONV_ROWS, :] * w_ref[kk:kk + 1, :]
        y = _layer_norm(acc, g_ref[...], b_ref[...])
        cu_ref[r0:r0 + CONV_ROWS, :] = (y * jax.nn.sigmoid(y)).astype(BF16)


def _conv(nb, seq, u, u_meta, dw_kernel, dw_bias, ln_g, ln_b):
    tc = CONV_TILE
    nt = seq // tc
    per = tc // CONV_HALO
    const = lambda b, t: (0, 0)
    return pl.pallas_call(
        _conv_kernel,
        grid=(nb, nt),
        in_specs=[
            pl.BlockSpec((tc, CONV_WIDTH), lambda b, t: (b * nt + t, 0)),
            pl.BlockSpec((CONV_HALO, CONV_WIDTH),
                         lambda b, t: (jnp.maximum((b * nt + t) * per - 1, 0), 0)),
            pl.BlockSpec((CONV_HALO, CONV_WIDTH), const),
            pl.BlockSpec((CONV_K + 1, CONV_WIDTH), const),
            pl.BlockSpec((1, CONV_WIDTH), const),
            pl.BlockSpec((1, CONV_WIDTH), const),
            pl.BlockSpec((1, CONV_WIDTH), const),
        ],
        out_specs=pl.BlockSpec((tc, CONV_WIDTH), lambda b, t: (b * nt + t, 0)),
        out_shape=jax.ShapeDtypeStruct((nb * seq, CONV_WIDTH), BF16),
        scratch_shapes=[
            pltpu.VMEM((CONV_HALO + tc, CONV_WIDTH), F32),
            pltpu.VMEM((SUBLANES - 1, CONV_HALO + tc - SUBLANES, CONV_WIDTH), F32),
        ],
        compiler_params=_params(("arbitrary", "arbitrary")),
        name="conv",
    )(u, u, u_meta, dw_kernel, dw_bias, ln_g, ln_b)


def _merge_kernel(x_ref, o_ref, cu_ref, sg_ref, lg_ref, lb_ref, wa_ref, wc_ref, bc_ref,
                  wo_ref, bo_ref, g1_ref, b1_ref, h1_ref):
    h = _layer_norm(x_ref[...], lg_ref[...], lb_ref[...])
    attn_d = jnp.dot(o_ref[...], wa_ref[...], preferred_element_type=F32)
    conv_d = jnp.dot(cu_ref[...], wc_ref[...], preferred_element_type=F32) + bc_ref[...]
    z = sg_ref[:, 0:D_MODEL] * attn_d + sg_ref[:, D_MODEL:2 * D_MODEL] * conv_d
    y = jnp.dot(z.astype(BF16), wo_ref[...], preferred_element_type=F32) + bo_ref[...]
    h1_ref[...] = _layer_norm(DEEPNORM_ALPHA * h + y, g1_ref[...], b1_ref[...])


def _merge(rows, x2, o, cu, sg, ln_g, ln_b, wa, wc, bc, wo, bo, g1, b1):
    tile = ROW_TILE
    row = lambda i: (i, 0)
    const = lambda i: (0, 0)
    vec = pl.BlockSpec((1, D_MODEL), const)
    mat = pl.BlockSpec((D_MODEL, D_MODEL), const)
    return pl.pallas_call(
        _merge_kernel,
        grid=(rows // tile,),
        in_specs=[
            pl.BlockSpec((tile, D_MODEL), row),
            pl.BlockSpec((tile, ATTN_WIDTH), row),
            pl.BlockSpec((tile, CONV_WIDTH), row),
            pl.BlockSpec((tile, 2 * D_MODEL), row),
            vec, vec, mat, mat, vec, mat, vec, vec, vec,
        ],
        out_specs=pl.BlockSpec((tile, D_MODEL), row),
        out_shape=jax.ShapeDtypeStruct((rows, D_MODEL), F32),
        compiler_params=_params(("arbitrary",)),
        name="merge",
    )(x2, o, cu, sg, ln_g, ln_b, wa, wc, bc, wo, bo, g1, b1)


def _candidate_pairs():
    return [(a, b) for a in range(PEER_TOPK) for b in range(PEER_TOPK // (a + 1))]


def _sorting_network(n):
    pairs = []
    p = 1
    while p < n:
        k = p
        while k >= 1:
            for j in range(k % p, n - k, 2 * k):
                for i in range(min(k, n - j - k)):
                    if (i + j) // (2 * p) == (i + j + k) // (2 * p):
                        pairs.append((i + j, i + j + k))
            k //= 2
        p *= 2
    return pairs


def _top_values(s, count):
    groups = s.shape[0] // SUBLANES
    assert groups & (groups - 1) == 0
    cols = [s[g * SUBLANES:(g + 1) * SUBLANES, :] for g in range(groups)]
    for i, j in _sorting_network(groups):
        hi, lo = jnp.maximum(cols[i], cols[j]), jnp.minimum(cols[i], cols[j])
        cols[i], cols[j] = hi, lo
    vals = []
    for r in range(count):
        mx = jnp.max(cols[0], axis=0, keepdims=True)
        vals.append(mx)
        hit = cols[0] == mx
        need = count - 1 - r
        for k in range(min(groups - 1, need)):
            cols[k] = jnp.where(hit, cols[k + 1], cols[k])
        if need >= groups:
            cols[groups - 1] = jnp.where(hit, -jnp.inf, cols[groups - 1])
    return vals


def _paired_bf16_words(x):
    bits = pltpu.bitcast(x.astype(BF16).astype(F32), jnp.uint32)
    return bits | lax.shift_right_logical(bits, jnp.uint32(16))


def _route_kernel(h_ref, wqt_ref, sk_ref, r2_ref, e2_ref, n1_ref, w1_ref):
    hb = h_ref[...].astype(BF16)
    nt = (((1,), (1,)), ((), ()))
    qt = lax.dot_general(wqt_ref[...], hb, nt, preferred_element_type=F32).astype(BF16)
    half = PEER_DKEY // 2
    tok = hb.shape[0]
    for hh in range(PEER_HEADS):
        s1 = jnp.dot(sk_ref[2 * hh], qt[2 * hh * half:(2 * hh + 1) * half, :],
                     preferred_element_type=F32)
        s2 = jnp.dot(sk_ref[2 * hh + 1], qt[(2 * hh + 1) * half:(2 * hh + 2) * half, :],
                     preferred_element_type=F32)
        top1 = _top_values(s1, PEER_TOPK)
        top2 = _top_values(s2, PEER_TOPK)
        rank2 = jnp.zeros_like(s2)
        for r in range(PEER_TOPK):
            rank2 = rank2 + jnp.where(s2 < top2[r], 1.0, 0.0)
        pairs = _candidate_pairs()
        cands = [top1[a] + top2[b] for a, b in pairs]
        pad = CAND_ROWS - len(cands)
        cand = jnp.concatenate(cands + [jnp.full((pad, tok), -jnp.inf, F32)], axis=0)
        best = _top_values(cand, PEER_TOPK)
        tau = best[PEER_TOPK - 1]
        z = jnp.ones_like(tau)
        for r in range(1, PEER_TOPK):
            z = z + jnp.exp(best[r] - best[0])
        n1 = jnp.zeros_like(s1)
        for a in range(PEER_TOPK):
            count = jnp.zeros_like(tau)
            for (pa, _), c in zip(pairs, cands):
                if pa == a:
                    count = count + jnp.where(c >= tau, 1.0, 0.0)
            n1 = jnp.where(s1 == top1[a], count, n1)
        r2_ref[hh] = rank2.astype(BF16)
        e2_ref[hh] = jnp.exp(s2 - top2[0]).astype(BF16)
        n1_ref[hh] = _paired_bf16_words(n1)
        w1_ref[hh] = _paired_bf16_words(jnp.exp(s1 - top1[0]) * (1.0 / z))


def _route(rows, h1, wqt, sk):
    tile = ROUTE_TILE
    blk = pl.BlockSpec((PEER_HEADS, PEER_KEYS, tile), lambda i: (0, 0, i))
    shape = (PEER_HEADS, PEER_KEYS, rows)
    return pl.pallas_call(
        _route_kernel,
        grid=(rows // tile,),
        in_specs=[
            pl.BlockSpec((tile, D_MODEL), lambda i: (i, 0)),
            pl.BlockSpec((PEER_HEADS * PEER_DKEY, D_MODEL), lambda i: (0, 0)),
            pl.BlockSpec((2 * PEER_HEADS, PEER_KEYS, PEER_DKEY // 2), lambda i: (0, 0, 0)),
        ],
        out_specs=[blk, blk, blk, blk],
        out_shape=[
            jax.ShapeDtypeStruct(shape, BF16),
            jax.ShapeDtypeStruct(shape, BF16),
            jax.ShapeDtypeStruct(shape, jnp.uint32),
            jax.ShapeDtypeStruct(shape, jnp.uint32),
        ],
        compiler_params=_params(("arbitrary",)),
        name="route",
    )(h1, wqt, sk)


def _peer_kernel(h_ref, u_ref, vt_ref, r2_ref, e2_ref, n1_ref, w1_ref, g2_ref, b2_ref, out_ref,
                 ht_ref, acc_ref, coef_ref):
    e = pl.program_id(1)
    ne = pl.num_programs(1)
    tok = h_ref.shape[0]
    n_first = PEER_EXP // PEER_KEYS

    @pl.when(e == 0)
    def _():
        ht_ref[...] = h_ref[...].T.astype(BF16)
        acc_ref[...] = jnp.zeros_like(acc_ref)

    act = jnp.dot(u_ref[...], ht_ref[...], preferred_element_type=F32)
    act = (0.5 * act * (1.0 + lax.erf(act * (2.0 ** -0.5)))).astype(BF16)
    row0 = pl.multiple_of(e * n_first, n_first)
    n_rows = [n1_ref[hh, pl.ds(row0, n_first), :] for hh in range(PEER_HEADS)]
    w_rows = [w1_ref[hh, pl.ds(row0, n_first), :] for hh in range(PEER_HEADS)]

    def packed_row(rows, ii):
        word_tile = jnp.broadcast_to(rows[ii:ii + 1, :], (n_first, tok))
        return pltpu.bitcast(word_tile, BF16)

    for ii in range(n_first):
        n_b = [packed_row(n_rows[hh], ii) for hh in range(PEER_HEADS)]
        w_b = [packed_row(w_rows[hh], ii) for hh in range(PEER_HEADS)]
        for c in range(PEER_KEYS // BF16_ROWS):
            js = slice(c * BF16_ROWS, (c + 1) * BF16_ROWS)
            gate = jnp.zeros((BF16_ROWS, tok), BF16)
            for hh in range(PEER_HEADS):
                e2 = e2_ref[hh, js, :]
                gate = gate + jnp.where(r2_ref[hh, js, :] < n_b[hh], e2 * w_b[hh], jnp.zeros_like(e2))
            r = ii * PEER_KEYS + c * BF16_ROWS
            coef_ref[r:r + BF16_ROWS, :] = gate * act[r:r + BF16_ROWS, :]
    acc_ref[...] += jnp.dot(vt_ref[...], coef_ref[...], preferred_element_type=F32)

    @pl.when(e == ne - 1)
    def _():
        y = DEEPNORM_ALPHA * h_ref[...] + acc_ref[...].T
        out_ref[...] = _layer_norm(y, g2_ref[...], b2_ref[...])


def _peer(rows, h1, u_bf, vt_bf, r2, e2, n1, w1, g2, b2):
    tok = PEER_TOK
    sel = pl.BlockSpec((PEER_HEADS, PEER_KEYS, tok), lambda t, e: (0, 0, t))
    return pl.pallas_call(
        _peer_kernel,
        grid=(rows // tok, PEER_N // PEER_EXP),
        in_specs=[
            pl.BlockSpec((tok, D_MODEL), lambda t, e: (t, 0)),
            pl.BlockSpec((PEER_EXP, D_MODEL), lambda t, e: (e, 0)),
            pl.BlockSpec((D_MODEL, PEER_EXP), lambda t, e: (0, e)),
            sel, sel, sel, sel,
            pl.BlockSpec((1, D_MODEL), lambda t, e: (0, 0)),
            pl.BlockSpec((1, D_MODEL), lambda t, e: (0, 0)),
        ],
        out_specs=pl.BlockSpec((tok, D_MODEL), lambda t, e: (t, 0)),
        out_shape=jax.ShapeDtypeStruct((rows, D_MODEL), F32),
        scratch_shapes=[
            pltpu.VMEM((D_MODEL, tok), BF16),
            pltpu.VMEM((D_MODEL, tok), F32),
            pltpu.VMEM((PEER_EXP, tok), BF16),
        ],
        compiler_params=_params(("arbitrary", "arbitrary")),
        name="peer",
    )(h1, u_bf, vt_bf, r2, e2, n1, w1, g2, b2)


def kernel(x, meta, ln_in_g, ln_in_b, w_in, b_in, lambda_q1, lambda_k1, lambda_q2, lambda_k2,
           subln_g, w_attn_proj, dw_kernel, dw_bias, conv_ln_g, conv_ln_b, w_conv_proj,
           b_conv_proj, w_out, b_out, ln1_g, ln1_b, peer_wq, peer_subkeys, peer_u, peer_v,
           ln2_g, ln2_b):
    nb, seq, _ = x.shape
    rows = nb * seq
    assert w_in.shape[0] == DEPTH and seq % PEER_TOK == 0 and rows % PEER_TOK == 0
    vec = lambda a: a.reshape(1, -1)
    x2 = x.reshape(rows, D_MODEL)
    w_bf = w_in[0].astype(BF16)
    bias = vec(b_in[0])
    g_in, b_ln = vec(ln_in_g), vec(ln_in_b)

    cos_t, sa_t, sb_t = _rope_tables(N_META + seq)
    frame_tabs = [t[N_META:] for t in (cos_t, sa_t, sb_t)]
    meta_tabs = [t[:META_PAD] for t in (cos_t, sa_t, sb_t)]
    meta_rows = jnp.pad(meta, ((0, META_PAD - N_META), (0, 0)))

    q, k, vt, u, sg = _in_proj(rows, seq, ROW_TILE, x2, g_in, b_ln, w_bf, bias, *frame_tabs)
    _, k_meta, vt_meta, u_meta, _ = _in_proj(META_PAD, META_PAD, META_PAD, meta_rows, g_in, b_ln,
                                             w_bf, bias, *meta_tabs)

    o = _attention(nb, seq, q, k, vt, k_meta, vt_meta, vec(lambda_q1[0]), vec(lambda_k1[0]),
                   vec(lambda_q2[0]), vec(lambda_k2[0]), vec(subln_g[0]))
    dw = jnp.pad(dw_kernel[0], ((0, 1), (0, 0)))
    cu = _conv(nb, seq, u, u_meta, dw, vec(dw_bias[0]), vec(conv_ln_g[0]), vec(conv_ln_b[0]))
    h1 = _merge(rows, x2, o, cu, sg, g_in, b_ln, w_attn_proj[0].astype(BF16),
                w_conv_proj[0].astype(BF16), vec(b_conv_proj[0]), w_out[0].astype(BF16),
                vec(b_out[0]), vec(ln1_g[0]), vec(ln1_b[0]))

    wqt = peer_wq[0].T.astype(BF16)
    sk = peer_subkeys[0].reshape(2 * PEER_HEADS, PEER_KEYS, PEER_DKEY // 2).astype(BF16)
    r2, e2, n1, w1 = _route(rows, h1, wqt, sk)
    h2 = _peer(rows, h1, peer_u[0].astype(BF16), peer_v[0].T.astype(BF16), r2, e2, n1, w1,
               vec(ln2_g[0]), vec(ln2_b[0]))
    return h2.reshape(nb, seq, D_MODEL)
```

```python
import functools
import math

import jax
import jax.numpy as jnp
from jax import lax
from jax.experimental import pallas as pl
from jax.experimental.pallas import tpu as pltpu

D_MODEL = 1024
CHUNK = 64
N_META = 16
N_HEADS = 8
HEAD_DIM = 64
V_HEAD_DIM = 2 * HEAD_DIM
QK_COLS = N_HEADS * 2 * HEAD_DIM
ATTN_WIDTH = N_HEADS * V_HEAD_DIM
ROPE_DIMS = HEAD_DIM // 4
ROPE_THETA = 500000.0
NEG_INF = -1e30
CONV_WIDTH = D_MODEL
CONV_K = 31
PEER_HEADS = 8
PEER_KEYS = 128
PEER_N = PEER_KEYS * PEER_KEYS
PEER_DKEY = 256
PEER_TOPK = 16
LN_EPS = 1e-5
DEPTH = 1
DEEPNORM_ALPHA = (2 * DEPTH) ** 0.25
LAM_INIT = 0.8 - 0.6 * math.exp(-0.3 * 0)
LOG2_E = math.log2(math.e)

LANES = 128
SUBLANES = 8
CONV_ROWS = 128
BF16_ROWS = 16
CAND_ROWS = 64
META_PAD = 128
VMEM_LIMIT = 56 * 1024 * 1024

ROW_TILE = 256
ATTN_TILE = 256
ATTN_KEYS = 128
CONV_TILE = 256
CONV_HALO = 32
ROUTE_TILE = 256
PEER_TOK = 512
PEER_EXP = 1024

F32 = jnp.float32
BF16 = jnp.bfloat16


def _layer_norm(x, g, b):
    mu = jnp.mean(x, axis=-1, keepdims=True)
    xc = x - mu
    var = jnp.mean(xc * xc, axis=-1, keepdims=True)
    return xc * lax.rsqrt(var + LN_EPS) * g + b


def _params(sem, vmem=VMEM_LIMIT):
    return pltpu.CompilerParams(dimension_semantics=sem, vmem_limit_bytes=vmem)


def _in_proj_kernel(x_ref, g_ref, b_ref, w_ref, bias_ref, cos_ref, sa_ref, sb_ref,
                    q_ref, k_ref, vt_ref, u_ref, sg_ref):
    hb = _layer_norm(x_ref[...], g_ref[...], b_ref[...]).astype(BF16)

    def proj(c0, n):
        return (jnp.dot(hb, w_ref[:, c0:c0 + n], preferred_element_type=F32)
                + bias_ref[:, c0:c0 + n])

    cos, sa, sb = cos_ref[...], sa_ref[...], sb_ref[...]

    def rope_store(t, out_ref, scale):
        for hh in range(QK_COLS // LANES):
            th = t[:, hh * LANES:(hh + 1) * LANES]
            r = (th * cos + pltpu.roll(th, LANES - ROPE_DIMS // 2, 1) * sa
                 + pltpu.roll(th, ROPE_DIMS // 2, 1) * sb)
            out_ref[:, hh * LANES:(hh + 1) * LANES] = (r * scale).astype(BF16)

    rope_store(proj(0, QK_COLS), q_ref, HEAD_DIM ** -0.5 * LOG2_E)
    rope_store(proj(QK_COLS, QK_COLS), k_ref, 1.0)
    vt_ref[0] = proj(2 * QK_COLS, ATTN_WIDTH).T.astype(BF16)
    c0 = 2 * QK_COLS + ATTN_WIDTH
    a = proj(c0, CONV_WIDTH)
    gt = proj(c0 + CONV_WIDTH, CONV_WIDTH)
    u_ref[...] = a * jax.nn.sigmoid(gt)
    sg_ref[...] = jax.nn.sigmoid(proj(c0 + 2 * CONV_WIDTH, 2 * D_MODEL))


def _in_proj(rows, seq, tile, x2, ln_g, ln_b, w_bf, bias, cos_t, sa_t, sb_t):
    n_seq = seq // tile
    nb = rows // seq
    in_cols = w_bf.shape[1]
    row = lambda i: (i, 0)
    const = lambda i: (0, 0)
    tab = lambda i: (i % n_seq, 0)
    return pl.pallas_call(
        _in_proj_kernel,
        grid=(rows // tile,),
        in_specs=[
            pl.BlockSpec((tile, D_MODEL), row),
            pl.BlockSpec((1, D_MODEL), const),
            pl.BlockSpec((1, D_MODEL), const),
            pl.BlockSpec((D_MODEL, in_cols), const, pipeline_mode=pl.Buffered(1)),
            pl.BlockSpec((1, in_cols), const),
            pl.BlockSpec((tile, LANES), tab),
            pl.BlockSpec((tile, LANES), tab),
            pl.BlockSpec((tile, LANES), tab),
        ],
        out_specs=[
            pl.BlockSpec((tile, QK_COLS), row),
            pl.BlockSpec((tile, QK_COLS), row),
            pl.BlockSpec((1, ATTN_WIDTH, tile), lambda i: (i // n_seq, 0, i % n_seq)),
            pl.BlockSpec((tile, CONV_WIDTH), row),
            pl.BlockSpec((tile, 2 * D_MODEL), row),
        ],
        out_shape=[
            jax.ShapeDtypeStruct((rows, QK_COLS), BF16),
            jax.ShapeDtypeStruct((rows, QK_COLS), BF16),
            jax.ShapeDtypeStruct((nb, ATTN_WIDTH, seq), BF16),
            jax.ShapeDtypeStruct((rows, CONV_WIDTH), F32),
            jax.ShapeDtypeStruct((rows, 2 * D_MODEL), F32),
        ],
        compiler_params=_params(("arbitrary",)),
        name="in_proj",
    )(x2, ln_g, ln_b, w_bf, bias, cos_t, sa_t, sb_t)


def _rope_tables(length):
    pos = jnp.arange(length, dtype=F32)
    inv = ROPE_THETA ** (-jnp.arange(0, ROPE_DIMS, 2, dtype=F32) / ROPE_DIMS)
    ang = pos[:, None] * inv[None, :]
    cos, sin = jnp.cos(ang), jnp.sin(ang)
    half = ROPE_DIMS // 2
    ones = jnp.ones((length, HEAD_DIM - ROPE_DIMS), F32)
    zeros = jnp.zeros((length, HEAD_DIM - ROPE_DIMS), F32)
    zh = jnp.zeros((length, half), F32)
    cos_h = jnp.concatenate([cos, cos, ones], axis=1)
    sa_h = jnp.concatenate([-sin, zh, zeros], axis=1)
    sb_h = jnp.concatenate([zh, sin, zeros], axis=1)
    rep = LANES // HEAD_DIM
    return (jnp.tile(cos_h, (1, rep)), jnp.tile(sa_h, (1, rep)), jnp.tile(sb_h, (1, rep)))


def _attn_kernel(lq1_ref, lk1_ref, lq2_ref, lk2_ref, subg_ref, q_ref, k_ref, vt_ref, km_ref, vtm_ref,
                 o_ref, qm_ref, m_ref, l_ref, acc_ref):
    qi = pl.program_id(1)
    tq = q_ref.shape[0]
    dim = lax.broadcasted_iota(jnp.int32, (LANES, tq), 0)
    maps = [(hh, m) for hh in range(N_HEADS) for m in range(2)]

    for hh in range(N_HEADS):
        qt = q_ref[:, hh * LANES:(hh + 1) * LANES].astype(F32).T
        qm_ref[2 * hh] = jnp.where(dim < HEAD_DIM, qt, 0.0).astype(BF16)
        qm_ref[2 * hh + 1] = jnp.where(dim >= HEAD_DIM, qt, 0.0).astype(BF16)

    def scores(keys, c):
        return jnp.dot(keys, qm_ref[c], preferred_element_type=F32)

    meta_ok = lax.broadcasted_iota(jnp.int32, (META_PAD, tq), 0) < N_META
    for hh, m in maps:
        c = 2 * hh + m
        s = jnp.where(meta_ok, scores(km_ref[:, hh * LANES:(hh + 1) * LANES], c), NEG_INF)
        mx = jnp.max(s, axis=0, keepdims=True)
        p = jnp.exp2(s - mx)
        m_ref[c] = mx
        l_ref[c] = jnp.sum(p, axis=0, keepdims=True)
        acc_ref[c] = jnp.dot(vtm_ref[0, hh * LANES:(hh + 1) * LANES, :], p.astype(BF16),
                             preferred_element_type=F32)

    tk = ATTN_KEYS
    per_q = tq // tk

    def tile_update(j, mask):
        start = pl.multiple_of(j * tk, tk)
        for hh, m in maps:
            c = 2 * hh + m
            s = scores(k_ref[pl.ds(start, tk), hh * LANES:(hh + 1) * LANES], c)
            if mask is not None:
                s = jnp.where(mask, s, NEG_INF)
            m_old = m_ref[c]
            m_new = jnp.maximum(m_old, jnp.max(s, axis=0, keepdims=True))
            alpha = jnp.exp2(m_old - m_new)
            p = jnp.exp2(s - m_new)
            m_ref[c] = m_new
            l_ref[c] = alpha * l_ref[c] + jnp.sum(p, axis=0, keepdims=True)
            acc_ref[c] = alpha * acc_ref[c] + jnp.dot(
                vt_ref[0, hh * LANES:(hh + 1) * LANES, pl.ds(start, tk)], p.astype(BF16),
                preferred_element_type=F32)

    def body(j, carry):
        tile_update(j, None)
        return carry

    lax.fori_loop(0, qi * per_q, body, 0)
    qc = lax.broadcasted_iota(jnp.int32, (tk, tq), 1) // CHUNK
    for d in range(per_q):
        kc = (lax.broadcasted_iota(jnp.int32, (tk, tq), 0) + d * tk) // CHUNK
        tile_update(qi * per_q + d, kc <= qc)

    lam = (jnp.exp(jnp.sum(lq1_ref[...] * lk1_ref[...], axis=1, keepdims=True))
           - jnp.exp(jnp.sum(lq2_ref[...] * lk2_ref[...], axis=1, keepdims=True)) + LAM_INIT)
    for hh in range(N_HEADS):
        c = 2 * hh
        ot = acc_ref[c] / l_ref[c] - lam * (acc_ref[c + 1] / l_ref[c + 1])
        ot = ot * lax.rsqrt(jnp.mean(ot * ot, axis=0, keepdims=True) + LN_EPS)
        o_ref[:, hh * LANES:(hh + 1) * LANES] = (
            ot.T * subg_ref[...] * (1.0 - LAM_INIT)).astype(BF16)


def _attention(nb, seq, q, k, vt, k_meta, vt_meta, lq1, lk1, lq2, lk2, subln_g):
    tq = ATTN_TILE
    nq = seq // tq
    n_maps = 2 * N_HEADS
    vec = lambda b, i: (0, 0)
    return pl.pallas_call(
        _attn_kernel,
        grid=(nb, nq),
        in_specs=[
            pl.BlockSpec((1, HEAD_DIM), vec),
            pl.BlockSpec((1, HEAD_DIM), vec),
            pl.BlockSpec((1, HEAD_DIM), vec),
            pl.BlockSpec((1, HEAD_DIM), vec),
            pl.BlockSpec((1, V_HEAD_DIM), vec),
            pl.BlockSpec((tq, QK_COLS), lambda b, i: (b * nq + i, 0)),
            pl.BlockSpec((seq, QK_COLS), lambda b, i: (b, 0)),
            pl.BlockSpec((1, ATTN_WIDTH, seq), lambda b, i: (b, 0, 0)),
            pl.BlockSpec((META_PAD, QK_COLS), vec),
            pl.BlockSpec((1, ATTN_WIDTH, META_PAD), lambda b, i: (0, 0, 0)),
        ],
        out_specs=pl.BlockSpec((tq, ATTN_WIDTH), lambda b, i: (b * nq + i, 0)),
        out_shape=jax.ShapeDtypeStruct((nb * seq, ATTN_WIDTH), BF16),
        scratch_shapes=[
            pltpu.VMEM((n_maps, LANES, tq), BF16),
            pltpu.VMEM((n_maps, 1, tq), F32),
            pltpu.VMEM((n_maps, 1, tq), F32),
            pltpu.VMEM((n_maps, V_HEAD_DIM, tq), F32),
        ],
        compiler_params=_params(("arbitrary", "arbitrary")),
        name="attn",
    )(lq1, lk1, lq2, lk2, subln_g, q, k, vt, k_meta, vt_meta)


def _conv_kernel(u_ref, prev_ref, um_ref, w_ref, bias_ref, g_ref, b_ref, cu_ref, pad_ref, sh_ref):
    t = pl.program_id(1)
    tc = u_ref.shape[0]
    pad_ref[CONV_HALO:CONV_HALO + tc, :] = u_ref[...]

    @pl.when(t == 0)
    def _():
        pad_ref[0:CONV_HALO - N_META, :] = jnp.zeros((CONV_HALO - N_META, CONV_WIDTH), F32)
        pad_ref[CONV_HALO - N_META:CONV_HALO, :] = um_ref[0:N_META, :]

    @pl.when(t > 0)
    def _():
        pad_ref[0:CONV_HALO, :] = prev_ref[...]

    first = CONV_HALO - (CONV_K - 1)
    span = tc + (first + CONV_K - 1) // SUBLANES * SUBLANES - SUBLANES
    for o in range(1, SUBLANES):
        sh_ref[o - 1] = pad_ref[o:o + span, :]

    for r0 in range(0, tc, CONV_ROWS):
        acc = jnp.zeros((CONV_ROWS, CONV_WIDTH), F32) + bias_ref[...]
        for kk in range(CONV_K):
            o = (first + kk) % SUBLANES
            base = r0 + first + kk - o
            src = pad_ref if o == 0 else sh_ref.at[o - 1]
            acc = acc + src[base:base + CONV_ROWS, :] * w_ref[kk:kk + 1, :]
        y = _layer_norm(acc, g_ref[...], b_ref[...])
        cu_ref[r0:r0 + CONV_ROWS, :] = (y * jax.nn.sigmoid(y)).astype(BF16)


def _conv(nb, seq, u, u_meta, dw_kernel, dw_bias, ln_g, ln_b):
    tc = CONV_TILE
    nt = seq // tc
    per = tc // CONV_HALO
    const = lambda b, t: (0, 0)
    return pl.pallas_call(
        _conv_kernel,
        grid=(nb, nt),
        in_specs=[
            pl.BlockSpec((tc, CONV_WIDTH), lambda b, t: (b * nt + t, 0)),
            pl.BlockSpec((CONV_HALO, CONV_WIDTH),
                         lambda b, t: (jnp.maximum((b * nt + t) * per - 1, 0), 0)),
            pl.BlockSpec((CONV_HALO, CONV_WIDTH), const),
            pl.BlockSpec((CONV_K + 1, CONV_WIDTH), const),
            pl.BlockSpec((1, CONV_WIDTH), const),
            pl.BlockSpec((1, CONV_WIDTH), const),
            pl.BlockSpec((1, CONV_WIDTH), const),
        ],
        out_specs=pl.BlockSpec((tc, CONV_WIDTH), lambda b, t: (b * nt + t, 0)),
        out_shape=jax.ShapeDtypeStruct((nb * seq, CONV_WIDTH), BF16),
        scratch_shapes=[
            pltpu.VMEM((CONV_HALO + tc, CONV_WIDTH), F32),
            pltpu.VMEM((SUBLANES - 1, CONV_HALO + tc - SUBLANES, CONV_WIDTH), F32),
        ],
        compiler_params=_params(("arbitrary", "arbitrary")),
        name="conv",
    )(u, u, u_meta, dw_kernel, dw_bias, ln_g, ln_b)


def _merge_kernel(x_ref, o_ref, cu_ref, sg_ref, lg_ref, lb_ref, wa_ref, wc_ref, bc_ref,
                  wo_ref, bo_ref, g1_ref, b1_ref, h1_ref):
    h = _layer_norm(x_ref[...], lg_ref[...], lb_ref[...])
    attn_d = jnp.dot(o_ref[...], wa_ref[...], preferred_element_type=F32)
    conv_d = jnp.dot(cu_ref[...], wc_ref[...], preferred_element_type=F32) + bc_ref[...]
    z = sg_ref[:, 0:D_MODEL] * attn_d + sg_ref[:, D_MODEL:2 * D_MODEL] * conv_d
    y = jnp.dot(z.astype(BF16), wo_ref[...], preferred_element_type=F32) + bo_ref[...]
    h1_ref[...] = _layer_norm(DEEPNORM_ALPHA * h + y, g1_ref[...], b1_ref[...])


def _merge(rows, x2, o, cu, sg, ln_g, ln_b, wa, wc, bc, wo, bo, g1, b1):
    tile = ROW_TILE
    row = lambda i: (i, 0)
    const = lambda i: (0, 0)
    vec = pl.BlockSpec((1, D_MODEL), const)
    mat = pl.BlockSpec((D_MODEL, D_MODEL), const)
    return pl.pallas_call(
        _merge_kernel,
        grid=(rows // tile,),
        in_specs=[
            pl.BlockSpec((tile, D_MODEL), row),
            pl.BlockSpec((tile, ATTN_WIDTH), row),
            pl.BlockSpec((tile, CONV_WIDTH), row),
            pl.BlockSpec((tile, 2 * D_MODEL), row),
            vec, vec, mat, mat, vec, mat, vec, vec, vec,
        ],
        out_specs=pl.BlockSpec((tile, D_MODEL), row),
        out_shape=jax.ShapeDtypeStruct((rows, D_MODEL), F32),
        compiler_params=_params(("arbitrary",)),
        name="merge",
    )(x2, o, cu, sg, ln_g, ln_b, wa, wc, bc, wo, bo, g1, b1)


def _candidate_pairs():
    return [(a, b) for a in range(PEER_TOPK) for b in range(PEER_TOPK // (a + 1))]


def _sorting_network(n):
    pairs = []
    p = 1
    while p < n:
        k = p
        while k >= 1:
            for j in range(k % p, n - k, 2 * k):
                for i in range(min(k, n - j - k)):
                    if (i + j) // (2 * p) == (i + j + k) // (2 * p):
                        pairs.append((i + j, i + j + k))
            k //= 2
        p *= 2
    return pairs


def _top_values(s, count):
    groups = s.shape[0] // SUBLANES
    assert groups & (groups - 1) == 0
    cols = [s[g * SUBLANES:(g + 1) * SUBLANES, :] for g in range(groups)]
    for i, j in _sorting_network(groups):
        hi, lo = jnp.maximum(cols[i], cols[j]), jnp.minimum(cols[i], cols[j])
        cols[i], cols[j] = hi, lo
    vals = []
    for r in range(count):
        mx = jnp.max(cols[0], axis=0, keepdims=True)
        vals.append(mx)
        hit = cols[0] == mx
        need = count - 1 - r
        for k in range(min(groups - 1, need)):
            cols[k] = jnp.where(hit, cols[k + 1], cols[k])
        if need >= groups:
            cols[groups - 1] = jnp.where(hit, -jnp.inf, cols[groups - 1])
    return vals


def _paired_bf16_words(x):
    bits = pltpu.bitcast(x.astype(BF16).astype(F32), jnp.uint32)
    return bits | lax.shift_right_logical(bits, jnp.uint32(16))


def _route_kernel(h_ref, wqt_ref, sk_ref, r2_ref, e2_ref, n1_ref, w1_ref):
    hb = h_ref[...].astype(BF16)
    nt = (((1,), (1,)), ((), ()))
    qt = lax.dot_general(wqt_ref[...], hb, nt, preferred_element_type=F32).astype(BF16)
    half = PEER_DKEY // 2
    tok = hb.shape[0]
    for hh in range(PEER_HEADS):
        s1 = jnp.dot(sk_ref[2 * hh], qt[2 * hh * half:(2 * hh + 1) * half, :],
                     preferred_element_type=F32)
        s2 = jnp.dot(sk_ref[2 * hh + 1], qt[(2 * hh + 1) * half:(2 * hh + 2) * half, :],
                     preferred_element_type=F32)
        top1 = _top_values(s1, PEER_TOPK)
        top2 = _top_values(s2, PEER_TOPK)
        rank2 = jnp.zeros_like(s2)
        for r in range(PEER_TOPK):
            rank2 = rank2 + jnp.where(s2 < top2[r], 1.0, 0.0)
        pairs = _candidate_pairs()
        cands = [top1[a] + top2[b] for a, b in pairs]
        pad = CAND_ROWS - len(cands)
        cand = jnp.concatenate(cands + [jnp.full((pad, tok), -jnp.inf, F32)], axis=0)
        best = _top_values(cand, PEER_TOPK)
        tau = best[PEER_TOPK - 1]
        z = jnp.ones_like(tau)
        for r in range(1, PEER_TOPK):
            z = z + jnp.exp(best[r] - best[0])
        n1 = jnp.zeros_like(s1)
        for a in range(PEER_TOPK):
            count = jnp.zeros_like(tau)
            for (pa, _), c in zip(pairs, cands):
                if pa == a:
                    count = count + jnp.where(c >= tau, 1.0, 0.0)
            n1 = jnp.where(s1 == top1[a], count, n1)
        r2_ref[hh] = rank2.astype(BF16)
        e2_ref[hh] = jnp.exp(s2 - top2[0]).astype(BF16)
        n1_ref[hh] = _paired_bf16_words(n1)
        w1_ref[hh] = _paired_bf16_words(jnp.exp(s1 - top1[0]) * (1.0 / z))


def _route(rows, h1, wqt, sk):
    tile = ROUTE_TILE
    blk = pl.BlockSpec((PEER_HEADS, PEER_KEYS, tile), lambda i: (0, 0, i))
    shape = (PEER_HEADS, PEER_KEYS, rows)
    return pl.pallas_call(
        _route_kernel,
        grid=(rows // tile,),
        in_specs=[
            pl.BlockSpec((tile, D_MODEL), lambda i: (i, 0)),
            pl.BlockSpec((PEER_HEADS * PEER_DKEY, D_MODEL), lambda i: (0, 0)),
            pl.BlockSpec((2 * PEER_HEADS, PEER_KEYS, PEER_DKEY // 2), lambda i: (0, 0, 0)),
        ],
        out_specs=[blk, blk, blk, blk],
        out_shape=[
            jax.ShapeDtypeStruct(shape, BF16),
            jax.ShapeDtypeStruct(shape, BF16),
            jax.ShapeDtypeStruct(shape, jnp.uint32),
            jax.ShapeDtypeStruct(shape, jnp.uint32),
        ],
        compiler_params=_params(("arbitrary",)),
        name="route",
    )(h1, wqt, sk)


def _peer_kernel(h_ref, u_ref, vt_ref, r2_ref, e2_ref, n1_ref, w1_ref, g2_ref, b2_ref, out_ref,
                 ht_ref, acc_ref, coef_ref):
    e = pl.program_id(1)
    ne = pl.num_programs(1)
    tok = h_ref.shape[0]
    n_first = PEER_EXP // PEER_KEYS

    @pl.when(e == 0)
    def _():
        ht_ref[...] = h_ref[...].T.astype(BF16)
        acc_ref[...] = jnp.zeros_like(acc_ref)

    row0 = pl.multiple_of(e * n_first, n_first)
    n_rows = [n1_ref[hh, pl.ds(row0, n_first), :] for hh in range(PEER_HEADS)]
    w_rows = [w1_ref[hh, pl.ds(row0, n_first), :] for hh in range(PEER_HEADS)]

    def packed_row(rows, ii):
        word_tile = jnp.broadcast_to(rows[ii:ii + 1, :], (n_first, tok))
        return pltpu.bitcast(word_tile, BF16)

    for ii in range(n_first):
        act = jnp.dot(u_ref[ii * PEER_KEYS:(ii + 1) * PEER_KEYS, :], ht_ref[...],
                      preferred_element_type=F32)
        act = (0.5 * act * (1.0 + lax.erf(act * (2.0 ** -0.5)))).astype(BF16)
        n_b = [packed_row(n_rows[hh], ii) for hh in range(PEER_HEADS)]
        w_b = [packed_row(w_rows[hh], ii) for hh in range(PEER_HEADS)]
        for c in range(PEER_KEYS // BF16_ROWS):
            js = slice(c * BF16_ROWS, (c + 1) * BF16_ROWS)
            gate = jnp.zeros((BF16_ROWS, tok), BF16)
            for hh in range(PEER_HEADS):
                e2 = e2_ref[hh, js, :]
                gate = gate + jnp.where(r2_ref[hh, js, :] < n_b[hh], e2 * w_b[hh], jnp.zeros_like(e2))
            r = ii * PEER_KEYS + c * BF16_ROWS
            coef_ref[r:r + BF16_ROWS, :] = gate * act[js, :]
    acc_ref[...] += jnp.dot(vt_ref[...], coef_ref[...], preferred_element_type=F32)

    @pl.when(e == ne - 1)
    def _():
        y = DEEPNORM_ALPHA * h_ref[...] + acc_ref[...].T
        out_ref[...] = _layer_norm(y, g2_ref[...], b2_ref[...])


def _peer(rows, h1, u_bf, vt_bf, r2, e2, n1, w1, g2, b2):
    tok = PEER_TOK
    sel = pl.BlockSpec((PEER_HEADS, PEER_KEYS, tok), lambda t, e: (0, 0, t))
    return pl.pallas_call(
        _peer_kernel,
        grid=(rows // tok, PEER_N // PEER_EXP),
        in_specs=[
            pl.BlockSpec((tok, D_MODEL), lambda t, e: (t, 0)),
            pl.BlockSpec((PEER_EXP, D_MODEL), lambda t, e: (e, 0)),
            pl.BlockSpec((D_MODEL, PEER_EXP), lambda t, e: (0, e)),
            sel, sel, sel, sel,
            pl.BlockSpec((1, D_MODEL), lambda t, e: (0, 0)),
            pl.BlockSpec((1, D_MODEL), lambda t, e: (0, 0)),
        ],
        out_specs=pl.BlockSpec((tok, D_MODEL), lambda t, e: (t, 0)),
        out_shape=jax.ShapeDtypeStruct((rows, D_MODEL), F32),
        scratch_shapes=[
            pltpu.VMEM((D_MODEL, tok), BF16),
            pltpu.VMEM((D_MODEL, tok), F32),
            pltpu.VMEM((PEER_EXP, tok), BF16),
        ],
        compiler_params=_params(("arbitrary", "arbitrary")),
        name="peer",
    )(h1, u_bf, vt_bf, r2, e2, n1, w1, g2, b2)


def kernel(x, meta, ln_in_g, ln_in_b, w_in, b_in, lambda_q1, lambda_k1, lambda_q2, lambda_k2,
           subln_g, w_attn_proj, dw_kernel, dw_bias, conv_ln_g, conv_ln_b, w_conv_proj,
           b_conv_proj, w_out, b_out, ln1_g, ln1_b, peer_wq, peer_subkeys, peer_u, peer_v,
           ln2_g, ln2_b):
    nb, seq, _ = x.shape
    rows = nb * seq
    assert w_in.shape[0] == DEPTH and seq % PEER_TOK == 0 and rows % PEER_TOK == 0
    vec = lambda a: a.reshape(1, -1)
    x2 = x.reshape(rows, D_MODEL)
    w_bf = w_in[0].astype(BF16)
    bias = vec(b_in[0])
    g_in, b_ln = vec(ln_in_g), vec(ln_in_b)

    cos_t, sa_t, sb_t = _rope_tables(N_META + seq)
    frame_tabs = [t[N_META:] for t in (cos_t, sa_t, sb_t)]
    meta_tabs = [t[:META_PAD] for t in (cos_t, sa_t, sb_t)]
    meta_rows = jnp.pad(meta, ((0, META_PAD - N_META), (0, 0)))

    q, k, vt, u, sg = _in_proj(rows, seq, ROW_TILE, x2, g_in, b_ln, w_bf, bias, *frame_tabs)
    _, k_meta, vt_meta, u_meta, _ = _in_proj(META_PAD, META_PAD, META_PAD, meta_rows, g_in, b_ln,
                                             w_bf, bias, *meta_tabs)

    o = _attention(nb, seq, q, k, vt, k_meta, vt_meta, vec(lambda_q1[0]), vec(lambda_k1[0]),
                   vec(lambda_q2[0]), vec(lambda_k2[0]), vec(subln_g[0]))
    dw = jnp.pad(dw_kernel[0], ((0, 1), (0, 0)))
    cu = _conv(nb, seq, u, u_meta, dw, vec(dw_bias[0]), vec(conv_ln_g[0]), vec(conv_ln_b[0]))
    h1 = _merge(rows, x2, o, cu, sg, g_in, b_ln, w_attn_proj[0].astype(BF16),
                w_conv_proj[0].astype(BF16), vec(b_conv_proj[0]), w_out[0].astype(BF16),
                vec(b_out[0]), vec(ln1_g[0]), vec(ln1_b[0]))

    wqt = peer_wq[0].T.astype(BF16)
    sk = peer_subkeys[0].reshape(2 * PEER_HEADS, PEER_KEYS, PEER_DKEY // 2).astype(BF16)
    r2, e2, n1, w1 = _route(rows, h1, wqt, sk)
    h2 = _peer(rows, h1, peer_u[0].astype(BF16), peer_v[0].T.astype(BF16), r2, e2, n1, w1,
               vec(ln2_g[0]), vec(ln2_b[0]))
    return h2.reshape(nb, seq, D_MODEL)
```

```python
import functools
import math

import jax
import jax.numpy as jnp
from jax import lax
from jax.experimental import pallas as pl
from jax.experimental.pallas import tpu as pltpu

D_MODEL = 1024
CHUNK = 64
N_META = 16
N_HEADS = 8
HEAD_DIM = 64
V_HEAD_DIM = 2 * HEAD_DIM
QK_COLS = N_HEADS * 2 * HEAD_DIM
ATTN_WIDTH = N_HEADS * V_HEAD_DIM
ROPE_DIMS = HEAD_DIM // 4
ROPE_THETA = 500000.0
NEG_INF = -1e30
CONV_WIDTH = D_MODEL
CONV_K = 31
PEER_HEADS = 8
PEER_KEYS = 128
PEER_N = PEER_KEYS * PEER_KEYS
PEER_DKEY = 256
PEER_TOPK = 16
LN_EPS = 1e-5
DEPTH = 1
DEEPNORM_ALPHA = (2 * DEPTH) ** 0.25
LAM_INIT = 0.8 - 0.6 * math.exp(-0.3 * 0)
LOG2_E = math.log2(math.e)

LANES = 128
SUBLANES = 8
CONV_ROWS = 128
BF16_ROWS = 16
CAND_ROWS = 64
META_PAD = 128
VMEM_LIMIT = 56 * 1024 * 1024

ROW_TILE = 256
ATTN_TILE = 256
ATTN_KEYS = 128
CONV_TILE = 256
CONV_HALO = 32
ROUTE_TILE = 512
PEER_TOK = 512
PEER_EXP = 1024

F32 = jnp.float32
BF16 = jnp.bfloat16


def _layer_norm(x, g, b):
    mu = jnp.mean(x, axis=-1, keepdims=True)
    xc = x - mu
    var = jnp.mean(xc * xc, axis=-1, keepdims=True)
    return xc * lax.rsqrt(var + LN_EPS) * g + b


def _params(sem, vmem=VMEM_LIMIT):
    return pltpu.CompilerParams(dimension_semantics=sem, vmem_limit_bytes=vmem)


def _in_proj_kernel(x_ref, g_ref, b_ref, w_ref, bias_ref, cos_ref, sa_ref, sb_ref,
                    q_ref, k_ref, vt_ref, u_ref, sg_ref):
    hb = _layer_norm(x_ref[...], g_ref[...], b_ref[...]).astype(BF16)

    def proj(c0, n):
        return (jnp.dot(hb, w_ref[:, c0:c0 + n], preferred_element_type=F32)
                + bias_ref[:, c0:c0 + n])

    cos, sa, sb = cos_ref[...], sa_ref[...], sb_ref[...]

    def rope_store(t, out_ref, scale):
        for hh in range(QK_COLS // LANES):
            th = t[:, hh * LANES:(hh + 1) * LANES]
            r = (th * cos + pltpu.roll(th, LANES - ROPE_DIMS // 2, 1) * sa
                 + pltpu.roll(th, ROPE_DIMS // 2, 1) * sb)
            out_ref[:, hh * LANES:(hh + 1) * LANES] = (r * scale).astype(BF16)

    rope_store(proj(0, QK_COLS), q_ref, HEAD_DIM ** -0.5 * LOG2_E)
    rope_store(proj(QK_COLS, QK_COLS), k_ref, 1.0)
    vt_ref[0] = proj(2 * QK_COLS, ATTN_WIDTH).T.astype(BF16)
    c0 = 2 * QK_COLS + ATTN_WIDTH
    a = proj(c0, CONV_WIDTH)
    gt = proj(c0 + CONV_WIDTH, CONV_WIDTH)
    u_ref[...] = a * jax.nn.sigmoid(gt)
    sg_ref[...] = jax.nn.sigmoid(proj(c0 + 2 * CONV_WIDTH, 2 * D_MODEL))


def _in_proj(rows, seq, tile, x2, ln_g, ln_b, w_bf, bias, cos_t, sa_t, sb_t):
    n_seq = seq // tile
    nb = rows // seq
    in_cols = w_bf.shape[1]
    row = lambda i: (i, 0)
    const = lambda i: (0, 0)
    tab = lambda i: (i % n_seq, 0)
    return pl.pallas_call(
        _in_proj_kernel,
        grid=(rows // tile,),
        in_specs=[
            pl.BlockSpec((tile, D_MODEL), row),
            pl.BlockSpec((1, D_MODEL), const),
            pl.BlockSpec((1, D_MODEL), const),
            pl.BlockSpec((D_MODEL, in_cols), const, pipeline_mode=pl.Buffered(1)),
            pl.BlockSpec((1, in_cols), const),
            pl.BlockSpec((tile, LANES), tab),
            pl.BlockSpec((tile, LANES), tab),
            pl.BlockSpec((tile, LANES), tab),
        ],
        out_specs=[
            pl.BlockSpec((tile, QK_COLS), row),
            pl.BlockSpec((tile, QK_COLS), row),
            pl.BlockSpec((1, ATTN_WIDTH, tile), lambda i: (i // n_seq, 0, i % n_seq)),
            pl.BlockSpec((tile, CONV_WIDTH), row),
            pl.BlockSpec((tile, 2 * D_MODEL), row),
        ],
        out_shape=[
            jax.ShapeDtypeStruct((rows, QK_COLS), BF16),
            jax.ShapeDtypeStruct((rows, QK_COLS), BF16),
            jax.ShapeDtypeStruct((nb, ATTN_WIDTH, seq), BF16),
            jax.ShapeDtypeStruct((rows, CONV_WIDTH), F32),
            jax.ShapeDtypeStruct((rows, 2 * D_MODEL), F32),
        ],
        compiler_params=_params(("arbitrary",)),
        name="in_proj",
    )(x2, ln_g, ln_b, w_bf, bias, cos_t, sa_t, sb_t)


def _rope_tables(length):
    pos = jnp.arange(length, dtype=F32)
    inv = ROPE_THETA ** (-jnp.arange(0, ROPE_DIMS, 2, dtype=F32) / ROPE_DIMS)
    ang = pos[:, None] * inv[None, :]
    cos, sin = jnp.cos(ang), jnp.sin(ang)
    half = ROPE_DIMS // 2
    ones = jnp.ones((length, HEAD_DIM - ROPE_DIMS), F32)
    zeros = jnp.zeros((length, HEAD_DIM - ROPE_DIMS), F32)
    zh = jnp.zeros((length, half), F32)
    cos_h = jnp.concatenate([cos, cos, ones], axis=1)
    sa_h = jnp.concatenate([-sin, zh, zeros], axis=1)
    sb_h = jnp.concatenate([zh, sin, zeros], axis=1)
    rep = LANES // HEAD_DIM
    return (jnp.tile(cos_h, (1, rep)), jnp.tile(sa_h, (1, rep)), jnp.tile(sb_h, (1, rep)))


def _attn_kernel(lq1_ref, lk1_ref, lq2_ref, lk2_ref, subg_ref, q_ref, k_ref, vt_ref, km_ref, vtm_ref,
                 o_ref, qm_ref, m_ref, l_ref, acc_ref):
    qi = pl.program_id(1)
    tq = q_ref.shape[0]
    dim = lax.broadcasted_iota(jnp.int32, (LANES, tq), 0)
    maps = [(hh, m) for hh in range(N_HEADS) for m in range(2)]

    for hh in range(N_HEADS):
        qt = q_ref[:, hh * LANES:(hh + 1) * LANES].astype(F32).T
        qm_ref[2 * hh] = jnp.where(dim < HEAD_DIM, qt, 0.0).astype(BF16)
        qm_ref[2 * hh + 1] = jnp.where(dim >= HEAD_DIM, qt, 0.0).astype(BF16)

    def scores(keys, c):
        return jnp.dot(keys, qm_ref[c], preferred_element_type=F32)

    meta_ok = lax.broadcasted_iota(jnp.int32, (META_PAD, tq), 0) < N_META
    for hh, m in maps:
        c = 2 * hh + m
        s = jnp.where(meta_ok, scores(km_ref[:, hh * LANES:(hh + 1) * LANES], c), NEG_INF)
        mx = jnp.max(s, axis=0, keepdims=True)
        p = jnp.exp2(s - mx)
        m_ref[c] = mx
        l_ref[c] = jnp.sum(p, axis=0, keepdims=True)
        acc_ref[c] = jnp.dot(vtm_ref[0, hh * LANES:(hh + 1) * LANES, :], p.astype(BF16),
                             preferred_element_type=F32)

    tk = ATTN_KEYS
    per_q = tq // tk

    def tile_update(j, mask):
        start = pl.multiple_of(j * tk, tk)
        for hh, m in maps:
            c = 2 * hh + m
            s = scores(k_ref[pl.ds(start, tk), hh * LANES:(hh + 1) * LANES], c)
            if mask is not None:
                s = jnp.where(mask, s, NEG_INF)
            m_old = m_ref[c]
            m_new = jnp.maximum(m_old, jnp.max(s, axis=0, keepdims=True))
            alpha = jnp.exp2(m_old - m_new)
            p = jnp.exp2(s - m_new)
            m_ref[c] = m_new
            l_ref[c] = alpha * l_ref[c] + jnp.sum(p, axis=0, keepdims=True)
            acc_ref[c] = alpha * acc_ref[c] + jnp.dot(
                vt_ref[0, hh * LANES:(hh + 1) * LANES, pl.ds(start, tk)], p.astype(BF16),
                preferred_element_type=F32)

    def body(j, carry):
        tile_update(j, None)
        return carry

    lax.fori_loop(0, qi * per_q, body, 0)
    qc = lax.broadcasted_iota(jnp.int32, (tk, tq), 1) // CHUNK
    for d in range(per_q):
        kc = (lax.broadcasted_iota(jnp.int32, (tk, tq), 0) + d * tk) // CHUNK
        tile_update(qi * per_q + d, kc <= qc)

    lam = (jnp.exp(jnp.sum(lq1_ref[...] * lk1_ref[...], axis=1, keepdims=True))
           - jnp.exp(jnp.sum(lq2_ref[...] * lk2_ref[...], axis=1, keepdims=True)) + LAM_INIT)
    for hh in range(N_HEADS):
        c = 2 * hh
        ot = acc_ref[c] / l_ref[c] - lam * (acc_ref[c + 1] / l_ref[c + 1])
        ot = ot * lax.rsqrt(jnp.mean(ot * ot, axis=0, keepdims=True) + LN_EPS)
        o_ref[:, hh * LANES:(hh + 1) * LANES] = (
            ot.T * subg_ref[...] * (1.0 - LAM_INIT)).astype(BF16)


def _attention(nb, seq, q, k, vt, k_meta, vt_meta, lq1, lk1, lq2, lk2, subln_g):
    tq = ATTN_TILE
    nq = seq // tq
    n_maps = 2 * N_HEADS
    vec = lambda b, i: (0, 0)
    return pl.pallas_call(
        _attn_kernel,
        grid=(nb, nq),
        in_specs=[
            pl.BlockSpec((1, HEAD_DIM), vec),
            pl.BlockSpec((1, HEAD_DIM), vec),
            pl.BlockSpec((1, HEAD_DIM), vec),
            pl.BlockSpec((1, HEAD_DIM), vec),
            pl.BlockSpec((1, V_HEAD_DIM), vec),
            pl.BlockSpec((tq, QK_COLS), lambda b, i: (b * nq + i, 0)),
            pl.BlockSpec((seq, QK_COLS), lambda b, i: (b, 0)),
            pl.BlockSpec((1, ATTN_WIDTH, seq), lambda b, i: (b, 0, 0)),
            pl.BlockSpec((META_PAD, QK_COLS), vec),
            pl.BlockSpec((1, ATTN_WIDTH, META_PAD), lambda b, i: (0, 0, 0)),
        ],
        out_specs=pl.BlockSpec((tq, ATTN_WIDTH), lambda b, i: (b * nq + i, 0)),
        out_shape=jax.ShapeDtypeStruct((nb * seq, ATTN_WIDTH), BF16),
        scratch_shapes=[
            pltpu.VMEM((n_maps, LANES, tq), BF16),
            pltpu.VMEM((n_maps, 1, tq), F32),
            pltpu.VMEM((n_maps, 1, tq), F32),
            pltpu.VMEM((n_maps, V_HEAD_DIM, tq), F32),
        ],
        compiler_params=_params(("arbitrary", "arbitrary")),
        name="attn",
    )(lq1, lk1, lq2, lk2, subln_g, q, k, vt, k_meta, vt_meta)


def _conv_kernel(u_ref, prev_ref, um_ref, w_ref, bias_ref, g_ref, b_ref, cu_ref, pad_ref, sh_ref):
    t = pl.program_id(1)
    tc = u_ref.shape[0]
    pad_ref[CONV_HALO:CONV_HALO + tc, :] = u_ref[...]

    @pl.when(t == 0)
    def _():
        pad_ref[0:CONV_HALO - N_META, :] = jnp.zeros((CONV_HALO - N_META, CONV_WIDTH), F32)
        pad_ref[CONV_HALO - N_META:CONV_HALO, :] = um_ref[0:N_META, :]

    @pl.when(t > 0)
    def _():
        pad_ref[0:CONV_HALO, :] = prev_ref[...]

    first = CONV_HALO - (CONV_K - 1)
    span = tc + (first + CONV_K - 1) // SUBLANES * SUBLANES - SUBLANES
    for o in range(1, SUBLANES):
        sh_ref[o - 1] = pad_ref[o:o + span, :]

    for r0 in range(0, tc, CONV_ROWS):
        acc = jnp.zeros((CONV_ROWS, CONV_WIDTH), F32) + bias_ref[...]
        for kk in range(CONV_K):
            o = (first + kk) % SUBLANES
            base = r0 + first + kk - o
            src = pad_ref if o == 0 else sh_ref.at[o - 1]
            acc = acc + src[base:base + CONV_ROWS, :] * w_ref[kk:kk + 1, :]
        y = _layer_norm(acc, g_ref[...], b_ref[...])
        cu_ref[r0:r0 + CONV_ROWS, :] = (y * jax.nn.sigmoid(y)).astype(BF16)


def _conv(nb, seq, u, u_meta, dw_kernel, dw_bias, ln_g, ln_b):
    tc = CONV_TILE
    nt = seq // tc
    per = tc // CONV_HALO
    const = lambda b, t: (0, 0)
    return pl.pallas_call(
        _conv_kernel,
        grid=(nb, nt),
        in_specs=[
            pl.BlockSpec((tc, CONV_WIDTH), lambda b, t: (b * nt + t, 0)),
            pl.BlockSpec((CONV_HALO, CONV_WIDTH),
                         lambda b, t: (jnp.maximum((b * nt + t) * per - 1, 0), 0)),
            pl.BlockSpec((CONV_HALO, CONV_WIDTH), const),
            pl.BlockSpec((CONV_K + 1, CONV_WIDTH), const),
            pl.BlockSpec((1, CONV_WIDTH), const),
            pl.BlockSpec((1, CONV_WIDTH), const),
            pl.BlockSpec((1, CONV_WIDTH), const),
        ],
        out_specs=pl.BlockSpec((tc, CONV_WIDTH), lambda b, t: (b * nt + t, 0)),
        out_shape=jax.ShapeDtypeStruct((nb * seq, CONV_WIDTH), BF16),
        scratch_shapes=[
            pltpu.VMEM((CONV_HALO + tc, CONV_WIDTH), F32),
            pltpu.VMEM((SUBLANES - 1, CONV_HALO + tc - SUBLANES, CONV_WIDTH), F32),
        ],
        compiler_params=_params(("arbitrary", "arbitrary")),
        name="conv",
    )(u, u, u_meta, dw_kernel, dw_bias, ln_g, ln_b)


def _merge_kernel(x_ref, o_ref, cu_ref, sg_ref, lg_ref, lb_ref, wa_ref, wc_ref, bc_ref,
                  wo_ref, bo_ref, g1_ref, b1_ref, h1_ref):
    h = _layer_norm(x_ref[...], lg_ref[...], lb_ref[...])
    attn_d = jnp.dot(o_ref[...], wa_ref[...], preferred_element_type=F32)
    conv_d = jnp.dot(cu_ref[...], wc_ref[...], preferred_element_type=F32) + bc_ref[...]
    z = sg_ref[:, 0:D_MODEL] * attn_d + sg_ref[:, D_MODEL:2 * D_MODEL] * conv_d
    y = jnp.dot(z.astype(BF16), wo_ref[...], preferred_element_type=F32) + bo_ref[...]
    h1_ref[...] = _layer_norm(DEEPNORM_ALPHA * h + y, g1_ref[...], b1_ref[...])


def _merge(rows, x2, o, cu, sg, ln_g, ln_b, wa, wc, bc, wo, bo, g1, b1):
    tile = ROW_TILE
    row = lambda i: (i, 0)
    const = lambda i: (0, 0)
    vec = pl.BlockSpec((1, D_MODEL), const)
    mat = pl.BlockSpec((D_MODEL, D_MODEL), const)
    return pl.pallas_call(
        _merge_kernel,
        grid=(rows // tile,),
        in_specs=[
            pl.BlockSpec((tile, D_MODEL), row),
            pl.BlockSpec((tile, ATTN_WIDTH), row),
            pl.BlockSpec((tile, CONV_WIDTH), row),
            pl.BlockSpec((tile, 2 * D_MODEL), row),
            vec, vec, mat, mat, vec, mat, vec, vec, vec,
        ],
        out_specs=pl.BlockSpec((tile, D_MODEL), row),
        out_shape=jax.ShapeDtypeStruct((rows, D_MODEL), F32),
        compiler_params=_params(("arbitrary",)),
        name="merge",
    )(x2, o, cu, sg, ln_g, ln_b, wa, wc, bc, wo, bo, g1, b1)


def _candidate_pairs():
    return [(a, b) for a in range(PEER_TOPK) for b in range(PEER_TOPK // (a + 1))]


def _sorting_network(n):
    pairs = []
    p = 1
    while p < n:
        k = p
        while k >= 1:
            for j in range(k % p, n - k, 2 * k):
                for i in range(min(k, n - j - k)):
                    if (i + j) // (2 * p) == (i + j + k) // (2 * p):
                        pairs.append((i + j, i + j + k))
            k //= 2
        p *= 2
    return pairs


def _top_values(s, count):
    groups = s.shape[0] // SUBLANES
    assert groups & (groups - 1) == 0
    cols = [s[g * SUBLANES:(g + 1) * SUBLANES, :] for g in range(groups)]
    for i, j in _sorting_network(groups):
        hi, lo = jnp.maximum(cols[i], cols[j]), jnp.minimum(cols[i], cols[j])
        cols[i], cols[j] = hi, lo
    vals = []
    for r in range(count):
        mx = jnp.max(cols[0], axis=0, keepdims=True)
        vals.append(mx)
        hit = cols[0] == mx
        need = count - 1 - r
        for k in range(min(groups - 1, need)):
            cols[k] = jnp.where(hit, cols[k + 1], cols[k])
        if need >= groups:
            cols[groups - 1] = jnp.where(hit, -jnp.inf, cols[groups - 1])
    return vals


def _paired_bf16_words(x):
    bits = pltpu.bitcast(x.astype(BF16).astype(F32), jnp.uint32)
    return bits | lax.shift_right_logical(bits, jnp.uint32(16))


def _route_kernel(h_ref, wqt_ref, sk_ref, r2_ref, e2_ref, n1_ref, w1_ref):
    hb = h_ref[...].astype(BF16)
    nt = (((1,), (1,)), ((), ()))
    qt = lax.dot_general(wqt_ref[...], hb, nt, preferred_element_type=F32).astype(BF16)
    half = PEER_DKEY // 2
    tok = hb.shape[0]
    for hh in range(PEER_HEADS):
        s1 = jnp.dot(sk_ref[2 * hh], qt[2 * hh * half:(2 * hh + 1) * half, :],
                     preferred_element_type=F32)
        s2 = jnp.dot(sk_ref[2 * hh + 1], qt[(2 * hh + 1) * half:(2 * hh + 2) * half, :],
                     preferred_element_type=F32)
        top1 = _top_values(s1, PEER_TOPK)
        top2 = _top_values(s2, PEER_TOPK)
        rank2 = jnp.zeros_like(s2)
        for r in range(PEER_TOPK):
            rank2 = rank2 + jnp.where(s2 < top2[r], 1.0, 0.0)
        pairs = _candidate_pairs()
        cands = [top1[a] + top2[b] for a, b in pairs]
        pad = CAND_ROWS - len(cands)
        cand = jnp.concatenate(cands + [jnp.full((pad, tok), -jnp.inf, F32)], axis=0)
        best = _top_values(cand, PEER_TOPK)
        tau = best[PEER_TOPK - 1]
        z = jnp.ones_like(tau)
        for r in range(1, PEER_TOPK):
            z = z + jnp.exp(best[r] - best[0])
        n1 = jnp.zeros_like(s1)
        for a in range(PEER_TOPK):
            count = jnp.zeros_like(tau)
            for (pa, _), c in zip(pairs, cands):
                if pa == a:
                    count = count + jnp.where(c >= tau, 1.0, 0.0)
            n1 = jnp.where(s1 == top1[a], count, n1)
        r2_ref[0, hh] = rank2.astype(BF16)
        e2_ref[0, hh] = jnp.exp(s2 - top2[0]).astype(BF16)
        n1_ref[0, hh] = _paired_bf16_words(n1)
        w1_ref[0, hh] = _paired_bf16_words(jnp.exp(s1 - top1[0]) * (1.0 / z))


def _route(rows, h1, wqt, sk):
    tile = ROUTE_TILE
    blk = pl.BlockSpec((1, PEER_HEADS, PEER_KEYS, tile), lambda i: (i, 0, 0, 0))
    shape = (rows // tile, PEER_HEADS, PEER_KEYS, tile)
    return pl.pallas_call(
        _route_kernel,
        grid=(rows // tile,),
        in_specs=[
            pl.BlockSpec((tile, D_MODEL), lambda i: (i, 0)),
            pl.BlockSpec((PEER_HEADS * PEER_DKEY, D_MODEL), lambda i: (0, 0)),
            pl.BlockSpec((2 * PEER_HEADS, PEER_KEYS, PEER_DKEY // 2), lambda i: (0, 0, 0)),
        ],
        out_specs=[blk, blk, blk, blk],
        out_shape=[
            jax.ShapeDtypeStruct(shape, BF16),
            jax.ShapeDtypeStruct(shape, BF16),
            jax.ShapeDtypeStruct(shape, jnp.uint32),
            jax.ShapeDtypeStruct(shape, jnp.uint32),
        ],
        compiler_params=_params(("arbitrary",)),
        name="route",
    )(h1, wqt, sk)


def _peer_kernel(h_ref, u_ref, vt_ref, r2_ref, e2_ref, n1_ref, w1_ref, g2_ref, b2_ref, out_ref,
                 ht_ref, acc_ref, coef_ref):
    e = pl.program_id(1)
    ne = pl.num_programs(1)
    tok = h_ref.shape[0]
    n_first = PEER_EXP // PEER_KEYS

    @pl.when(e == 0)
    def _():
        ht_ref[...] = h_ref[...].T.astype(BF16)
        acc_ref[...] = jnp.zeros_like(acc_ref)

    row0 = pl.multiple_of(e * n_first, n_first)
    n_rows = [n1_ref[0, hh, pl.ds(row0, n_first), :] for hh in range(PEER_HEADS)]
    w_rows = [w1_ref[0, hh, pl.ds(row0, n_first), :] for hh in range(PEER_HEADS)]

    def packed_row(rows, ii):
        word_tile = jnp.broadcast_to(rows[ii:ii + 1, :], (n_first, tok))
        return pltpu.bitcast(word_tile, BF16)

    for ii in range(n_first):
        act = jnp.dot(u_ref[ii * PEER_KEYS:(ii + 1) * PEER_KEYS, :], ht_ref[...],
                      preferred_element_type=F32)
        act = (0.5 * act * (1.0 + lax.erf(act * (2.0 ** -0.5)))).astype(BF16)
        n_b = [packed_row(n_rows[hh], ii) for hh in range(PEER_HEADS)]
        w_b = [packed_row(w_rows[hh], ii) for hh in range(PEER_HEADS)]
        for c in range(PEER_KEYS // BF16_ROWS):
            js = slice(c * BF16_ROWS, (c + 1) * BF16_ROWS)
            gate = jnp.zeros((BF16_ROWS, tok), BF16)
            for hh in range(PEER_HEADS):
                e2 = e2_ref[0, hh, js, :]
                gate = gate + jnp.where(r2_ref[0, hh, js, :] < n_b[hh], e2 * w_b[hh],
                                        jnp.zeros_like(e2))
            r = ii * PEER_KEYS + c * BF16_ROWS
            coef_ref[r:r + BF16_ROWS, :] = gate * act[js, :]
    acc_ref[...] += jnp.dot(vt_ref[0], coef_ref[...], preferred_element_type=F32)

    @pl.when(e == ne - 1)
    def _():
        y = DEEPNORM_ALPHA * h_ref[...] + acc_ref[...].T
        out_ref[...] = _layer_norm(y, g2_ref[...], b2_ref[...])


def _peer(rows, h1, u_bf, vt_bf, r2, e2, n1, w1, g2, b2):
    tok = PEER_TOK
    sel = pl.BlockSpec((1, PEER_HEADS, PEER_KEYS, tok), lambda t, e: (t, 0, 0, 0))
    return pl.pallas_call(
        _peer_kernel,
        grid=(rows // tok, PEER_N // PEER_EXP),
        in_specs=[
            pl.BlockSpec((tok, D_MODEL), lambda t, e: (t, 0)),
            pl.BlockSpec((PEER_EXP, D_MODEL), lambda t, e: (e, 0)),
            pl.BlockSpec((1, D_MODEL, PEER_EXP), lambda t, e: (e, 0, 0)),
            sel, sel, sel, sel,
            pl.BlockSpec((1, D_MODEL), lambda t, e: (0, 0)),
            pl.BlockSpec((1, D_MODEL), lambda t, e: (0, 0)),
        ],
        out_specs=pl.BlockSpec((tok, D_MODEL), lambda t, e: (t, 0)),
        out_shape=jax.ShapeDtypeStruct((rows, D_MODEL), F32),
        scratch_shapes=[
            pltpu.VMEM((D_MODEL, tok), BF16),
            pltpu.VMEM((D_MODEL, tok), F32),
            pltpu.VMEM((PEER_EXP, tok), BF16),
        ],
        compiler_params=_params(("arbitrary", "arbitrary")),
        name="peer",
    )(h1, u_bf, vt_bf, r2, e2, n1, w1, g2, b2)


def kernel(x, meta, ln_in_g, ln_in_b, w_in, b_in, lambda_q1, lambda_k1, lambda_q2, lambda_k2,
           subln_g, w_attn_proj, dw_kernel, dw_bias, conv_ln_g, conv_ln_b, w_conv_proj,
           b_conv_proj, w_out, b_out, ln1_g, ln1_b, peer_wq, peer_subkeys, peer_u, peer_v,
           ln2_g, ln2_b):
    nb, seq, _ = x.shape
    rows = nb * seq
    assert w_in.shape[0] == DEPTH and seq % PEER_TOK == 0 and rows % PEER_TOK == 0
    vec = lambda a: a.reshape(1, -1)
    x2 = x.reshape(rows, D_MODEL)
    w_bf = w_in[0].astype(BF16)
    bias = vec(b_in[0])
    g_in, b_ln = vec(ln_in_g), vec(ln_in_b)

    cos_t, sa_t, sb_t = _rope_tables(N_META + seq)
    frame_tabs = [t[N_META:] for t in (cos_t, sa_t, sb_t)]
    meta_tabs = [t[:META_PAD] for t in (cos_t, sa_t, sb_t)]
    meta_rows = jnp.pad(meta, ((0, META_PAD - N_META), (0, 0)))

    q, k, vt, u, sg = _in_proj(rows, seq, ROW_TILE, x2, g_in, b_ln, w_bf, bias, *frame_tabs)
    _, k_meta, vt_meta, u_meta, _ = _in_proj(META_PAD, META_PAD, META_PAD, meta_rows, g_in, b_ln,
                                             w_bf, bias, *meta_tabs)

    o = _attention(nb, seq, q, k, vt, k_meta, vt_meta, vec(lambda_q1[0]), vec(lambda_k1[0]),
                   vec(lambda_q2[0]), vec(lambda_k2[0]), vec(subln_g[0]))
    dw = jnp.pad(dw_kernel[0], ((0, 1), (0, 0)))
    cu = _conv(nb, seq, u, u_meta, dw, vec(dw_bias[0]), vec(conv_ln_g[0]), vec(conv_ln_b[0]))
    h1 = _merge(rows, x2, o, cu, sg, g_in, b_ln, w_attn_proj[0].astype(BF16),
                w_conv_proj[0].astype(BF16), vec(b_conv_proj[0]), w_out[0].astype(BF16),
                vec(b_out[0]), vec(ln1_g[0]), vec(ln1_b[0]))

    wqt = peer_wq[0].T.astype(BF16)
    sk = peer_subkeys[0].reshape(2 * PEER_HEADS, PEER_KEYS, PEER_DKEY // 2).astype(BF16)
    r2, e2, n1, w1 = _route(rows, h1, wqt, sk)
    vt_blocks = (peer_v[0].astype(BF16).reshape(PEER_N // PEER_EXP, PEER_EXP, D_MODEL)
                 .transpose(0, 2, 1))
    h2 = _peer(rows, h1, peer_u[0].astype(BF16), vt_blocks, r2, e2, n1, w1,
               vec(ln2_g[0]), vec(ln2_b[0]))
    return h2.reshape(nb, seq, D_MODEL)
```

```python
import math

import jax
import jax.numpy as jnp
from jax import lax
from jax.experimental import pallas as pl
from jax.experimental.pallas import tpu as pltpu

D_MODEL = 1024
CHUNK = 64
N_META = 16
N_HEADS = 8
HEAD_DIM = 64
V_HEAD_DIM = 2 * HEAD_DIM
QK_COLS = N_HEADS * 2 * HEAD_DIM
ATTN_WIDTH = N_HEADS * V_HEAD_DIM
ROPE_DIMS = HEAD_DIM // 4
ROPE_THETA = 500000.0
NEG_INF = -1e30
CONV_WIDTH = D_MODEL
CONV_K = 31
PEER_HEADS = 8
PEER_KEYS = 128
PEER_N = PEER_KEYS * PEER_KEYS
PEER_DKEY = 256
PEER_TOPK = 16
LN_EPS = 1e-5
DEPTH = 1
DEEPNORM_ALPHA = (2 * DEPTH) ** 0.25
LAM_INIT = 0.8 - 0.6 * math.exp(-0.3 * 0)
LOG2_E = math.log2(math.e)

LANES = 128
SUBLANES = 8
CONV_ROWS = 128
BF16_ROWS = 16
CAND_ROWS = 64
META_PAD = 128
V7X_VMEM_BYTES = 64 * 1024 * 1024
VMEM_LIMIT = V7X_VMEM_BYTES * 7 // 8

ROW_TILE = 512
ATTN_TILE = 256
ATTN_KEYS = 128
CONV_TILE = 512
CONV_HALO = 32
ROUTE_TILE = 512
PEER_TOK = 512
PEER_EXP = 2048

F32 = jnp.float32
BF16 = jnp.bfloat16


def _layer_norm(x, g, b):
    mu = jnp.mean(x, axis=-1, keepdims=True)
    xc = x - mu
    var = jnp.mean(xc * xc, axis=-1, keepdims=True)
    return xc * lax.rsqrt(var + LN_EPS) * g + b


def _params(sem, vmem=VMEM_LIMIT):
    return pltpu.CompilerParams(dimension_semantics=sem, vmem_limit_bytes=vmem)


def _in_proj_kernel(x_ref, g_ref, b_ref, w_ref, bias_ref, cos_ref, sa_ref, sb_ref,
                    q_ref, k_ref, vt_ref, u_ref, sg_ref):
    hb = _layer_norm(x_ref[...], g_ref[...], b_ref[...]).astype(BF16)

    def proj(c0, n):
        return (jnp.dot(hb, w_ref[:, c0:c0 + n], preferred_element_type=F32)
                + bias_ref[:, c0:c0 + n])

    cos, sa, sb = cos_ref[...], sa_ref[...], sb_ref[...]

    def rope_store(t, out_ref, scale):
        for hh in range(QK_COLS // LANES):
            th = t[:, hh * LANES:(hh + 1) * LANES]
            r = (th * cos + pltpu.roll(th, LANES - ROPE_DIMS // 2, 1) * sa
                 + pltpu.roll(th, ROPE_DIMS // 2, 1) * sb)
            out_ref[:, hh * LANES:(hh + 1) * LANES] = (r * scale).astype(BF16)

    rope_store(proj(0, QK_COLS), q_ref, HEAD_DIM ** -0.5 * LOG2_E)
    rope_store(proj(QK_COLS, QK_COLS), k_ref, 1.0)
    vt_ref[0] = proj(2 * QK_COLS, ATTN_WIDTH).T.astype(BF16)
    c0 = 2 * QK_COLS + ATTN_WIDTH
    a = proj(c0, CONV_WIDTH)
    gt = proj(c0 + CONV_WIDTH, CONV_WIDTH)
    u_ref[...] = a * jax.nn.sigmoid(gt)
    sg_ref[...] = jax.nn.sigmoid(proj(c0 + 2 * CONV_WIDTH, 2 * D_MODEL))


def _in_proj(rows, seq, tile, x2, ln_g, ln_b, w_bf, bias, cos_t, sa_t, sb_t):
    n_seq = seq // tile
    nb = rows // seq
    in_cols = w_bf.shape[1]
    row = lambda i: (i, 0)
    const = lambda i: (0, 0)
    tab = lambda i: (i % n_seq, 0)
    return pl.pallas_call(
        _in_proj_kernel,
        grid=(rows // tile,),
        in_specs=[
            pl.BlockSpec((tile, D_MODEL), row),
            pl.BlockSpec((1, D_MODEL), const),
            pl.BlockSpec((1, D_MODEL), const),
            pl.BlockSpec((D_MODEL, in_cols), const, pipeline_mode=pl.Buffered(1)),
            pl.BlockSpec((1, in_cols), const),
            pl.BlockSpec((tile, LANES), tab),
            pl.BlockSpec((tile, LANES), tab),
            pl.BlockSpec((tile, LANES), tab),
        ],
        out_specs=[
            pl.BlockSpec((tile, QK_COLS), row),
            pl.BlockSpec((tile, QK_COLS), row),
            pl.BlockSpec((1, ATTN_WIDTH, tile), lambda i: (i // n_seq, 0, i % n_seq)),
            pl.BlockSpec((tile, CONV_WIDTH), row),
            pl.BlockSpec((tile, 2 * D_MODEL), row),
        ],
        out_shape=[
            jax.ShapeDtypeStruct((rows, QK_COLS), BF16),
            jax.ShapeDtypeStruct((rows, QK_COLS), BF16),
            jax.ShapeDtypeStruct((nb, ATTN_WIDTH, seq), BF16),
            jax.ShapeDtypeStruct((rows, CONV_WIDTH), F32),
            jax.ShapeDtypeStruct((rows, 2 * D_MODEL), F32),
        ],
        compiler_params=_params(("arbitrary",)),
        name="in_proj",
    )(x2, ln_g, ln_b, w_bf, bias, cos_t, sa_t, sb_t)


def _rope_tables(length):
    pos = jnp.arange(length, dtype=F32)
    inv = ROPE_THETA ** (-jnp.arange(0, ROPE_DIMS, 2, dtype=F32) / ROPE_DIMS)
    ang = pos[:, None] * inv[None, :]
    cos, sin = jnp.cos(ang), jnp.sin(ang)
    half = ROPE_DIMS // 2
    ones = jnp.ones((length, HEAD_DIM - ROPE_DIMS), F32)
    zeros = jnp.zeros((length, HEAD_DIM - ROPE_DIMS), F32)
    zh = jnp.zeros((length, half), F32)
    cos_h = jnp.concatenate([cos, cos, ones], axis=1)
    sa_h = jnp.concatenate([-sin, zh, zeros], axis=1)
    sb_h = jnp.concatenate([zh, sin, zeros], axis=1)
    rep = LANES // HEAD_DIM
    return (jnp.tile(cos_h, (1, rep)), jnp.tile(sa_h, (1, rep)), jnp.tile(sb_h, (1, rep)))


def _attn_kernel(lq1_ref, lk1_ref, lq2_ref, lk2_ref, subg_ref, q_ref, k_ref, vt_ref, km_ref, vtm_ref,
                 o_ref, qm_ref, m_ref, l_ref, acc_ref):
    qi = pl.program_id(1)
    tq = q_ref.shape[0]
    dim = lax.broadcasted_iota(jnp.int32, (LANES, tq), 0)
    maps = [(hh, m) for hh in range(N_HEADS) for m in range(2)]

    for hh in range(N_HEADS):
        qt = q_ref[:, hh * LANES:(hh + 1) * LANES].astype(F32).T
        qm_ref[2 * hh] = jnp.where(dim < HEAD_DIM, qt, 0.0).astype(BF16)
        qm_ref[2 * hh + 1] = jnp.where(dim >= HEAD_DIM, qt, 0.0).astype(BF16)

    def scores(keys, c):
        return jnp.dot(keys, qm_ref[c], preferred_element_type=F32)

    meta_ok = lax.broadcasted_iota(jnp.int32, (META_PAD, tq), 0) < N_META
    for hh, m in maps:
        c = 2 * hh + m
        s = jnp.where(meta_ok, scores(km_ref[:, hh * LANES:(hh + 1) * LANES], c), NEG_INF)
        mx = jnp.max(s, axis=0, keepdims=True)
        p = jnp.exp2(s - mx)
        m_ref[c] = mx
        l_ref[c] = jnp.sum(p, axis=0, keepdims=True)
        acc_ref[c] = jnp.dot(vtm_ref[0, hh * LANES:(hh + 1) * LANES, :], p.astype(BF16),
                             preferred_element_type=F32)

    tk = ATTN_KEYS
    per_q = tq // tk

    def tile_update(j, mask):
        start = pl.multiple_of(j * tk, tk)
        for hh, m in maps:
            c = 2 * hh + m
            s = scores(k_ref[pl.ds(start, tk), hh * LANES:(hh + 1) * LANES], c)
            if mask is not None:
                s = jnp.where(mask, s, NEG_INF)
            m_old = m_ref[c]
            m_new = jnp.maximum(m_old, jnp.max(s, axis=0, keepdims=True))
            alpha = jnp.exp2(m_old - m_new)
            p = jnp.exp2(s - m_new)
            m_ref[c] = m_new
            l_ref[c] = alpha * l_ref[c] + jnp.sum(p, axis=0, keepdims=True)
            acc_ref[c] = alpha * acc_ref[c] + jnp.dot(
                vt_ref[0, hh * LANES:(hh + 1) * LANES, pl.ds(start, tk)], p.astype(BF16),
                preferred_element_type=F32)

    def body(j, carry):
        tile_update(j, None)
        return carry

    lax.fori_loop(0, qi * per_q, body, 0)
    qc = lax.broadcasted_iota(jnp.int32, (tk, tq), 1) // CHUNK
    for d in range(per_q):
        kc = (lax.broadcasted_iota(jnp.int32, (tk, tq), 0) + d * tk) // CHUNK
        tile_update(qi * per_q + d, kc <= qc)

    lam = (jnp.exp(jnp.sum(lq1_ref[...] * lk1_ref[...], axis=1, keepdims=True))
           - jnp.exp(jnp.sum(lq2_ref[...] * lk2_ref[...], axis=1, keepdims=True)) + LAM_INIT)
    for hh in range(N_HEADS):
        c = 2 * hh
        ot = acc_ref[c] / l_ref[c] - lam * (acc_ref[c + 1] / l_ref[c + 1])
        ot = ot * lax.rsqrt(jnp.mean(ot * ot, axis=0, keepdims=True) + LN_EPS)
        o_ref[:, hh * LANES:(hh + 1) * LANES] = (
            ot.T * subg_ref[...] * (1.0 - LAM_INIT)).astype(BF16)


def _attention(nb, seq, q, k, vt, k_meta, vt_meta, lq1, lk1, lq2, lk2, subln_g):
    tq = ATTN_TILE
    nq = seq // tq
    n_maps = 2 * N_HEADS
    vec = lambda b, i: (0, 0)
    return pl.pallas_call(
        _attn_kernel,
        grid=(nb, nq),
        in_specs=[
            pl.BlockSpec((1, HEAD_DIM), vec),
            pl.BlockSpec((1, HEAD_DIM), vec),
            pl.BlockSpec((1, HEAD_DIM), vec),
            pl.BlockSpec((1, HEAD_DIM), vec),
            pl.BlockSpec((1, V_HEAD_DIM), vec),
            pl.BlockSpec((tq, QK_COLS), lambda b, i: (b * nq + i, 0)),
            pl.BlockSpec((seq, QK_COLS), lambda b, i: (b, 0)),
            pl.BlockSpec((1, ATTN_WIDTH, seq), lambda b, i: (b, 0, 0)),
            pl.BlockSpec((META_PAD, QK_COLS), vec),
            pl.BlockSpec((1, ATTN_WIDTH, META_PAD), lambda b, i: (0, 0, 0)),
        ],
        out_specs=pl.BlockSpec((tq, ATTN_WIDTH), lambda b, i: (b * nq + i, 0)),
        out_shape=jax.ShapeDtypeStruct((nb * seq, ATTN_WIDTH), BF16),
        scratch_shapes=[
            pltpu.VMEM((n_maps, LANES, tq), BF16),
            pltpu.VMEM((n_maps, 1, tq), F32),
            pltpu.VMEM((n_maps, 1, tq), F32),
            pltpu.VMEM((n_maps, V_HEAD_DIM, tq), F32),
        ],
        compiler_params=_params(("arbitrary", "arbitrary")),
        name="attn",
    )(lq1, lk1, lq2, lk2, subln_g, q, k, vt, k_meta, vt_meta)


def _conv_kernel(u_ref, prev_ref, um_ref, w_ref, bias_ref, g_ref, b_ref, cu_ref, pad_ref, sh_ref):
    t = pl.program_id(1)
    tc = u_ref.shape[0]
    pad_ref[CONV_HALO:CONV_HALO + tc, :] = u_ref[...]

    @pl.when(t == 0)
    def _():
        pad_ref[0:CONV_HALO - N_META, :] = jnp.zeros((CONV_HALO - N_META, CONV_WIDTH), F32)
        pad_ref[CONV_HALO - N_META:CONV_HALO, :] = um_ref[0:N_META, :]

    @pl.when(t > 0)
    def _():
        pad_ref[0:CONV_HALO, :] = prev_ref[...]

    first = CONV_HALO - (CONV_K - 1)
    span = tc + (first + CONV_K - 1) // SUBLANES * SUBLANES - SUBLANES
    for o in range(1, SUBLANES):
        sh_ref[o - 1] = pad_ref[o:o + span, :]

    for r0 in range(0, tc, CONV_ROWS):
        acc = jnp.zeros((CONV_ROWS, CONV_WIDTH), F32) + bias_ref[...]
        for kk in range(CONV_K):
            o = (first + kk) % SUBLANES
            base = r0 + first + kk - o
            src = pad_ref if o == 0 else sh_ref.at[o - 1]
            acc = acc + src[base:base + CONV_ROWS, :] * w_ref[kk:kk + 1, :]
        y = _layer_norm(acc, g_ref[...], b_ref[...])
        cu_ref[r0:r0 + CONV_ROWS, :] = (y * jax.nn.sigmoid(y)).astype(BF16)


def _conv(nb, seq, u, u_meta, dw_kernel, dw_bias, ln_g, ln_b):
    tc = CONV_TILE
    nt = seq // tc
    per = tc // CONV_HALO
    const = lambda b, t: (0, 0)
    return pl.pallas_call(
        _conv_kernel,
        grid=(nb, nt),
        in_specs=[
            pl.BlockSpec((tc, CONV_WIDTH), lambda b, t: (b * nt + t, 0)),
            pl.BlockSpec((CONV_HALO, CONV_WIDTH),
                         lambda b, t: (jnp.maximum((b * nt + t) * per - 1, 0), 0)),
            pl.BlockSpec((CONV_HALO, CONV_WIDTH), const),
            pl.BlockSpec((CONV_K + 1, CONV_WIDTH), const),
            pl.BlockSpec((1, CONV_WIDTH), const),
            pl.BlockSpec((1, CONV_WIDTH), const),
            pl.BlockSpec((1, CONV_WIDTH), const),
        ],
        out_specs=pl.BlockSpec((tc, CONV_WIDTH), lambda b, t: (b * nt + t, 0)),
        out_shape=jax.ShapeDtypeStruct((nb * seq, CONV_WIDTH), BF16),
        scratch_shapes=[
            pltpu.VMEM((CONV_HALO + tc, CONV_WIDTH), F32),
            pltpu.VMEM((SUBLANES - 1, CONV_HALO + tc - SUBLANES, CONV_WIDTH), F32),
        ],
        compiler_params=_params(("arbitrary", "arbitrary")),
        name="conv",
    )(u, u, u_meta, dw_kernel, dw_bias, ln_g, ln_b)


def _merge_kernel(x_ref, o_ref, cu_ref, sg_ref, lg_ref, lb_ref, wa_ref, wc_ref, bc_ref,
                  wo_ref, bo_ref, g1_ref, b1_ref, h1_ref):
    h = _layer_norm(x_ref[...], lg_ref[...], lb_ref[...])
    attn_d = jnp.dot(o_ref[...], wa_ref[...], preferred_element_type=F32)
    conv_d = jnp.dot(cu_ref[...], wc_ref[...], preferred_element_type=F32) + bc_ref[...]
    z = sg_ref[:, 0:D_MODEL] * attn_d + sg_ref[:, D_MODEL:2 * D_MODEL] * conv_d
    y = jnp.dot(z.astype(BF16), wo_ref[...], preferred_element_type=F32) + bo_ref[...]
    h1_ref[...] = _layer_norm(DEEPNORM_ALPHA * h + y, g1_ref[...], b1_ref[...])


def _merge(rows, x2, o, cu, sg, ln_g, ln_b, wa, wc, bc, wo, bo, g1, b1):
    tile = ROW_TILE
    row = lambda i: (i, 0)
    const = lambda i: (0, 0)
    vec = pl.BlockSpec((1, D_MODEL), const)
    mat = pl.BlockSpec((D_MODEL, D_MODEL), const)
    return pl.pallas_call(
        _merge_kernel,
        grid=(rows // tile,),
        in_specs=[
            pl.BlockSpec((tile, D_MODEL), row),
            pl.BlockSpec((tile, ATTN_WIDTH), row),
            pl.BlockSpec((tile, CONV_WIDTH), row),
            pl.BlockSpec((tile, 2 * D_MODEL), row),
            vec, vec, mat, mat, vec, mat, vec, vec, vec,
        ],
        out_specs=pl.BlockSpec((tile, D_MODEL), row),
        out_shape=jax.ShapeDtypeStruct((rows, D_MODEL), F32),
        compiler_params=_params(("arbitrary",)),
        name="merge",
    )(x2, o, cu, sg, ln_g, ln_b, wa, wc, bc, wo, bo, g1, b1)


def _candidate_pairs():
    return [(a, b) for a in range(PEER_TOPK) for b in range(PEER_TOPK // (a + 1))]


def _sorting_network(n):
    pairs = []
    p = 1
    while p < n:
        k = p
        while k >= 1:
            for j in range(k % p, n - k, 2 * k):
                for i in range(min(k, n - j - k)):
                    if (i + j) // (2 * p) == (i + j + k) // (2 * p):
                        pairs.append((i + j, i + j + k))
            k //= 2
        p *= 2
    return pairs


def _top_values(s, count):
    groups = s.shape[0] // SUBLANES
    assert groups & (groups - 1) == 0
    cols = [s[g * SUBLANES:(g + 1) * SUBLANES, :] for g in range(groups)]
    for i, j in _sorting_network(groups):
        hi, lo = jnp.maximum(cols[i], cols[j]), jnp.minimum(cols[i], cols[j])
        cols[i], cols[j] = hi, lo
    vals = []
    for r in range(count):
        mx = jnp.max(cols[0], axis=0, keepdims=True)
        vals.append(mx)
        hit = cols[0] == mx
        need = count - 1 - r
        for k in range(min(groups - 1, need)):
            cols[k] = jnp.where(hit, cols[k + 1], cols[k])
        if need >= groups:
            cols[groups - 1] = jnp.where(hit, -jnp.inf, cols[groups - 1])
    return vals


def _paired_bf16_words(x):
    bits = pltpu.bitcast(x.astype(BF16).astype(F32), jnp.uint32)
    return bits | lax.shift_right_logical(bits, jnp.uint32(16))


def _route_kernel(h_ref, wqt_ref, sk_ref, r2_ref, e2_ref, n1_ref, w1_ref):
    hb = h_ref[...].astype(BF16)
    nt = (((1,), (1,)), ((), ()))
    qt = lax.dot_general(wqt_ref[...], hb, nt, preferred_element_type=F32).astype(BF16)
    half = PEER_DKEY // 2
    tok = hb.shape[0]
    for hh in range(PEER_HEADS):
        s1 = jnp.dot(sk_ref[2 * hh], qt[2 * hh * half:(2 * hh + 1) * half, :],
                     preferred_element_type=F32)
        s2 = jnp.dot(sk_ref[2 * hh + 1], qt[(2 * hh + 1) * half:(2 * hh + 2) * half, :],
                     preferred_element_type=F32)
        top1 = _top_values(s1, PEER_TOPK)
        top2 = _top_values(s2, PEER_TOPK)
        rank2 = jnp.zeros_like(s2)
        for r in range(PEER_TOPK):
            rank2 = rank2 + jnp.where(s2 < top2[r], 1.0, 0.0)
        pairs = _candidate_pairs()
        cands = [top1[a] + top2[b] for a, b in pairs]
        pad = CAND_ROWS - len(cands)
        cand = jnp.concatenate(cands + [jnp.full((pad, tok), -jnp.inf, F32)], axis=0)
        best = _top_values(cand, PEER_TOPK)
        tau = best[PEER_TOPK - 1]
        z = jnp.ones_like(tau)
        for r in range(1, PEER_TOPK):
            z = z + jnp.exp(best[r] - best[0])
        n1 = jnp.zeros_like(s1)
        for a in range(PEER_TOPK):
            count = jnp.zeros_like(tau)
            for (pa, _), c in zip(pairs, cands):
                if pa == a:
                    count = count + jnp.where(c >= tau, 1.0, 0.0)
            n1 = jnp.where(s1 == top1[a], count, n1)
        r2_ref[0, hh] = rank2.astype(BF16)
        e2_ref[0, hh] = jnp.exp(s2 - top2[0]).astype(BF16)
        n1_ref[0, hh] = _paired_bf16_words(n1)
        w1_ref[0, hh] = _paired_bf16_words(jnp.exp(s1 - top1[0]) * (1.0 / z))


def _route(rows, h1, wqt, sk):
    tile = ROUTE_TILE
    blk = pl.BlockSpec((1, PEER_HEADS, PEER_KEYS, tile), lambda i: (i, 0, 0, 0))
    shape = (rows // tile, PEER_HEADS, PEER_KEYS, tile)
    return pl.pallas_call(
        _route_kernel,
        grid=(rows // tile,),
        in_specs=[
            pl.BlockSpec((tile, D_MODEL), lambda i: (i, 0)),
            pl.BlockSpec((PEER_HEADS * PEER_DKEY, D_MODEL), lambda i: (0, 0)),
            pl.BlockSpec((2 * PEER_HEADS, PEER_KEYS, PEER_DKEY // 2), lambda i: (0, 0, 0)),
        ],
        out_specs=[blk, blk, blk, blk],
        out_shape=[
            jax.ShapeDtypeStruct(shape, BF16),
            jax.ShapeDtypeStruct(shape, BF16),
            jax.ShapeDtypeStruct(shape, jnp.uint32),
            jax.ShapeDtypeStruct(shape, jnp.uint32),
        ],
        compiler_params=_params(("arbitrary",)),
        name="route",
    )(h1, wqt, sk)


def _peer_kernel(h_ref, u_ref, vt_ref, r2_ref, e2_ref, n1_ref, w1_ref, g2_ref, b2_ref, out_ref,
                 ht_ref, acc_ref, coef_ref):
    e = pl.program_id(1)
    ne = pl.num_programs(1)
    tok = h_ref.shape[0]
    n_first = PEER_EXP // PEER_KEYS

    @pl.when(e == 0)
    def _():
        ht_ref[...] = h_ref[...].T.astype(BF16)
        acc_ref[...] = jnp.zeros_like(acc_ref)

    row0 = pl.multiple_of(e * n_first, n_first)
    n_rows = [n1_ref[0, hh, pl.ds(row0, n_first), :] for hh in range(PEER_HEADS)]
    w_rows = [w1_ref[0, hh, pl.ds(row0, n_first), :] for hh in range(PEER_HEADS)]

    def packed_row(rows, ii):
        word_tile = jnp.broadcast_to(rows[ii:ii + 1, :], (SUBLANES, tok))
        return pltpu.bitcast(word_tile, BF16)

    for ii in range(n_first):
        act = jnp.dot(u_ref[ii * PEER_KEYS:(ii + 1) * PEER_KEYS, :], ht_ref[...],
                      preferred_element_type=F32)
        act = (0.5 * act * (1.0 + lax.erf(act * (2.0 ** -0.5)))).astype(BF16)
        n_b = [packed_row(n_rows[hh], ii) for hh in range(PEER_HEADS)]
        w_b = [packed_row(w_rows[hh], ii) for hh in range(PEER_HEADS)]
        for c in range(PEER_KEYS // BF16_ROWS):
            js = slice(c * BF16_ROWS, (c + 1) * BF16_ROWS)
            gate = jnp.zeros((BF16_ROWS, tok), BF16)
            for hh in range(PEER_HEADS):
                e2 = e2_ref[0, hh, js, :]
                gate = gate + jnp.where(r2_ref[0, hh, js, :] < n_b[hh], e2 * w_b[hh],
                                        jnp.zeros_like(e2))
            r = ii * PEER_KEYS + c * BF16_ROWS
            coef_ref[r:r + BF16_ROWS, :] = gate * act[js, :]
    acc_ref[...] += jnp.dot(vt_ref[0], coef_ref[...], preferred_element_type=F32)

    @pl.when(e == ne - 1)
    def _():
        y = DEEPNORM_ALPHA * h_ref[...] + acc_ref[...].T
        out_ref[...] = _layer_norm(y, g2_ref[...], b2_ref[...])


def _peer(rows, h1, u_bf, vt_bf, r2, e2, n1, w1, g2, b2):
    tok = PEER_TOK
    sel = pl.BlockSpec((1, PEER_HEADS, PEER_KEYS, tok), lambda t, e: (t, 0, 0, 0))
    return pl.pallas_call(
        _peer_kernel,
        grid=(rows // tok, PEER_N // PEER_EXP),
        in_specs=[
            pl.BlockSpec((tok, D_MODEL), lambda t, e: (t, 0)),
            pl.BlockSpec((PEER_EXP, D_MODEL), lambda t, e: (e, 0)),
            pl.BlockSpec((1, D_MODEL, PEER_EXP), lambda t, e: (e, 0, 0)),
            sel, sel, sel, sel,
            pl.BlockSpec((1, D_MODEL), lambda t, e: (0, 0)),
            pl.BlockSpec((1, D_MODEL), lambda t, e: (0, 0)),
        ],
        out_specs=pl.BlockSpec((tok, D_MODEL), lambda t, e: (t, 0)),
        out_shape=jax.ShapeDtypeStruct((rows, D_MODEL), F32),
        scratch_shapes=[
            pltpu.VMEM((D_MODEL, tok), BF16),
            pltpu.VMEM((D_MODEL, tok), F32),
            pltpu.VMEM((PEER_EXP, tok), BF16),
        ],
        compiler_params=_params(("arbitrary", "arbitrary")),
        name="peer",
    )(h1, u_bf, vt_bf, r2, e2, n1, w1, g2, b2)


def kernel(x, meta, ln_in_g, ln_in_b, w_in, b_in, lambda_q1, lambda_k1, lambda_q2, lambda_k2,
           subln_g, w_attn_proj, dw_kernel, dw_bias, conv_ln_g, conv_ln_b, w_conv_proj,
           b_conv_proj, w_out, b_out, ln1_g, ln1_b, peer_wq, peer_subkeys, peer_u, peer_v,
           ln2_g, ln2_b):
    nb, seq, _ = x.shape
    rows = nb * seq
    assert w_in.shape[0] == DEPTH and seq % PEER_TOK == 0 and PEER_TOK == ROUTE_TILE
    vec = lambda a: a.reshape(1, -1)
    x2 = x.reshape(rows, D_MODEL)
    w_bf = w_in[0].astype(BF16)
    bias = vec(b_in[0])
    g_in, b_ln = vec(ln_in_g), vec(ln_in_b)

    cos_t, sa_t, sb_t = _rope_tables(N_META + seq)
    frame_tabs = [t[N_META:] for t in (cos_t, sa_t, sb_t)]
    meta_tabs = [t[:META_PAD] for t in (cos_t, sa_t, sb_t)]
    meta_rows = jnp.pad(meta, ((0, META_PAD - N_META), (0, 0)))

    q, k, vt, u, sg = _in_proj(rows, seq, ROW_TILE, x2, g_in, b_ln, w_bf, bias, *frame_tabs)
    _, k_meta, vt_meta, u_meta, _ = _in_proj(META_PAD, META_PAD, META_PAD, meta_rows, g_in, b_ln,
                                             w_bf, bias, *meta_tabs)

    o = _attention(nb, seq, q, k, vt, k_meta, vt_meta, vec(lambda_q1[0]), vec(lambda_k1[0]),
                   vec(lambda_q2[0]), vec(lambda_k2[0]), vec(subln_g[0]))
    dw = jnp.pad(dw_kernel[0], ((0, 1), (0, 0)))
    cu = _conv(nb, seq, u, u_meta, dw, vec(dw_bias[0]), vec(conv_ln_g[0]), vec(conv_ln_b[0]))
    h1 = _merge(rows, x2, o, cu, sg, g_in, b_ln, w_attn_proj[0].astype(BF16),
                w_conv_proj[0].astype(BF16), vec(b_conv_proj[0]), w_out[0].astype(BF16),
                vec(b_out[0]), vec(ln1_g[0]), vec(ln1_b[0]))

    wqt = peer_wq[0].T.astype(BF16)
    sk = peer_subkeys[0].reshape(2 * PEER_HEADS, PEER_KEYS, PEER_DKEY // 2).astype(BF16)
    r2, e2, n1, w1 = _route(rows, h1, wqt, sk)
    vt_blocks = (peer_v[0].astype(BF16).reshape(PEER_N // PEER_EXP, PEER_EXP, D_MODEL)
                 .transpose(0, 2, 1))
    h2 = _peer(rows, h1, peer_u[0].astype(BF16), vt_blocks, r2, e2, n1, w1,
               vec(ln2_g[0]), vec(ln2_b[0]))
    return h2.reshape(nb, seq, D_MODEL)
```

```python
import math

import jax
import jax.numpy as jnp
from jax import lax
from jax.experimental import pallas as pl
from jax.experimental.pallas import tpu as pltpu

D_MODEL = 1024
CHUNK = 64
N_META = 16
N_HEADS = 8
HEAD_DIM = 64
V_HEAD_DIM = 2 * HEAD_DIM
QK_COLS = N_HEADS * 2 * HEAD_DIM
ATTN_WIDTH = N_HEADS * V_HEAD_DIM
ROPE_DIMS = HEAD_DIM // 4
ROPE_THETA = 500000.0
NEG_INF = -1e30
CONV_WIDTH = D_MODEL
CONV_K = 31
PEER_HEADS = 8
PEER_KEYS = 128
PEER_N = PEER_KEYS * PEER_KEYS
PEER_DKEY = 256
PEER_TOPK = 16
LN_EPS = 1e-5
DEPTH = 1
DEEPNORM_ALPHA = (2 * DEPTH) ** 0.25
LAM_INIT = 0.8 - 0.6 * math.exp(-0.3 * 0)
LOG2_E = math.log2(math.e)

LANES = 128
SUBLANES = 8
CONV_ROWS = 128
BF16_ROWS = 16
CAND_ROWS = 64
META_PAD = 128
V7X_VMEM_BYTES = 64 * 1024 * 1024
VMEM_LIMIT = V7X_VMEM_BYTES * 7 // 8

IN_PROJ_TILE = 256
MERGE_TILE = 512
ATTN_TILE = 256
ATTN_KEYS = 128
CONV_TILE = 512
CONV_HALO = 32
ROUTE_TILE = 512
PEER_TOK = 512
PEER_EXP = 2048

F32 = jnp.float32
BF16 = jnp.bfloat16


def _layer_norm(x, g, b):
    mu = jnp.mean(x, axis=-1, keepdims=True)
    xc = x - mu
    var = jnp.mean(xc * xc, axis=-1, keepdims=True)
    return xc * lax.rsqrt(var + LN_EPS) * g + b


def _params(sem, vmem=VMEM_LIMIT):
    return pltpu.CompilerParams(dimension_semantics=sem, vmem_limit_bytes=vmem)


def _in_proj_kernel(x_ref, g_ref, b_ref, w_ref, bias_ref, cos_ref, sa_ref, sb_ref,
                    q_ref, k_ref, vt_ref, u_ref, sg_ref):
    hb = _layer_norm(x_ref[...], g_ref[...], b_ref[...]).astype(BF16)

    def proj(c0, n):
        return (jnp.dot(hb, w_ref[:, c0:c0 + n], preferred_element_type=F32)
                + bias_ref[:, c0:c0 + n])

    cos, sa, sb = cos_ref[...], sa_ref[...], sb_ref[...]

    def rope_store(t, out_ref, scale):
        for hh in range(QK_COLS // LANES):
            th = t[:, hh * LANES:(hh + 1) * LANES]
            r = (th * cos + pltpu.roll(th, LANES - ROPE_DIMS // 2, 1) * sa
                 + pltpu.roll(th, ROPE_DIMS // 2, 1) * sb)
            out_ref[:, hh * LANES:(hh + 1) * LANES] = (r * scale).astype(BF16)

    rope_store(proj(0, QK_COLS), q_ref, HEAD_DIM ** -0.5 * LOG2_E)
    rope_store(proj(QK_COLS, QK_COLS), k_ref, 1.0)
    vt_ref[0] = proj(2 * QK_COLS, ATTN_WIDTH).T.astype(BF16)
    c0 = 2 * QK_COLS + ATTN_WIDTH
    a = proj(c0, CONV_WIDTH)
    gt = proj(c0 + CONV_WIDTH, CONV_WIDTH)
    u_ref[...] = a * jax.nn.sigmoid(gt)
    sg_ref[...] = jax.nn.sigmoid(proj(c0 + 2 * CONV_WIDTH, 2 * D_MODEL))


def _in_proj(rows, seq, tile, x2, ln_g, ln_b, w_bf, bias, cos_t, sa_t, sb_t):
    n_seq = seq // tile
    nb = rows // seq
    in_cols = w_bf.shape[1]
    row = lambda i: (i, 0)
    const = lambda i: (0, 0)
    tab = lambda i: (i % n_seq, 0)
    return pl.pallas_call(
        _in_proj_kernel,
        grid=(rows // tile,),
        in_specs=[
            pl.BlockSpec((tile, D_MODEL), row),
            pl.BlockSpec((1, D_MODEL), const),
            pl.BlockSpec((1, D_MODEL), const),
            pl.BlockSpec((D_MODEL, in_cols), const, pipeline_mode=pl.Buffered(1)),
            pl.BlockSpec((1, in_cols), const),
            pl.BlockSpec((tile, LANES), tab),
            pl.BlockSpec((tile, LANES), tab),
            pl.BlockSpec((tile, LANES), tab),
        ],
        out_specs=[
            pl.BlockSpec((tile, QK_COLS), row),
            pl.BlockSpec((tile, QK_COLS), row),
            pl.BlockSpec((1, ATTN_WIDTH, tile), lambda i: (i // n_seq, 0, i % n_seq)),
            pl.BlockSpec((tile, CONV_WIDTH), row),
            pl.BlockSpec((tile, 2 * D_MODEL), row),
        ],
        out_shape=[
            jax.ShapeDtypeStruct((rows, QK_COLS), BF16),
            jax.ShapeDtypeStruct((rows, QK_COLS), BF16),
            jax.ShapeDtypeStruct((nb, ATTN_WIDTH, seq), BF16),
            jax.ShapeDtypeStruct((rows, CONV_WIDTH), F32),
            jax.ShapeDtypeStruct((rows, 2 * D_MODEL), F32),
        ],
        compiler_params=_params(("arbitrary",)),
        name="in_proj",
    )(x2, ln_g, ln_b, w_bf, bias, cos_t, sa_t, sb_t)


def _rope_tables(length):
    pos = jnp.arange(length, dtype=F32)
    inv = ROPE_THETA ** (-jnp.arange(0, ROPE_DIMS, 2, dtype=F32) / ROPE_DIMS)
    ang = pos[:, None] * inv[None, :]
    cos, sin = jnp.cos(ang), jnp.sin(ang)
    half = ROPE_DIMS // 2
    ones = jnp.ones((length, HEAD_DIM - ROPE_DIMS), F32)
    zeros = jnp.zeros((length, HEAD_DIM - ROPE_DIMS), F32)
    zh = jnp.zeros((length, half), F32)
    cos_h = jnp.concatenate([cos, cos, ones], axis=1)
    sa_h = jnp.concatenate([-sin, zh, zeros], axis=1)
    sb_h = jnp.concatenate([zh, sin, zeros], axis=1)
    rep = LANES // HEAD_DIM
    return (jnp.tile(cos_h, (1, rep)), jnp.tile(sa_h, (1, rep)), jnp.tile(sb_h, (1, rep)))


def _attn_kernel(lq1_ref, lk1_ref, lq2_ref, lk2_ref, subg_ref, q_ref, k_ref, vt_ref, km_ref, vtm_ref,
                 o_ref, qm_ref, m_ref, l_ref, acc_ref):
    qi = pl.program_id(1)
    tq = q_ref.shape[0]
    dim = lax.broadcasted_iota(jnp.int32, (LANES, tq), 0)
    maps = [(hh, m) for hh in range(N_HEADS) for m in range(2)]

    for hh in range(N_HEADS):
        qt = q_ref[:, hh * LANES:(hh + 1) * LANES].astype(F32).T
        qm_ref[2 * hh] = jnp.where(dim < HEAD_DIM, qt, 0.0).astype(BF16)
        qm_ref[2 * hh + 1] = jnp.where(dim >= HEAD_DIM, qt, 0.0).astype(BF16)

    def scores(keys, c):
        return jnp.dot(keys, qm_ref[c], preferred_element_type=F32)

    meta_ok = lax.broadcasted_iota(jnp.int32, (META_PAD, tq), 0) < N_META
    for hh, m in maps:
        c = 2 * hh + m
        s = jnp.where(meta_ok, scores(km_ref[:, hh * LANES:(hh + 1) * LANES], c), NEG_INF)
        mx = jnp.max(s, axis=0, keepdims=True)
        p = jnp.exp2(s - mx)
        m_ref[c] = mx
        l_ref[c] = jnp.sum(p, axis=0, keepdims=True)
        acc_ref[c] = jnp.dot(vtm_ref[0, hh * LANES:(hh + 1) * LANES, :], p.astype(BF16),
                             preferred_element_type=F32)

    tk = ATTN_KEYS
    per_q = tq // tk

    def tile_update(j, mask):
        start = pl.multiple_of(j * tk, tk)
        for hh, m in maps:
            c = 2 * hh + m
            s = scores(k_ref[pl.ds(start, tk), hh * LANES:(hh + 1) * LANES], c)
            if mask is not None:
                s = jnp.where(mask, s, NEG_INF)
            m_old = m_ref[c]
            m_new = jnp.maximum(m_old, jnp.max(s, axis=0, keepdims=True))
            alpha = jnp.exp2(m_old - m_new)
            p = jnp.exp2(s - m_new)
            m_ref[c] = m_new
            l_ref[c] = alpha * l_ref[c] + jnp.sum(p, axis=0, keepdims=True)
            acc_ref[c] = alpha * acc_ref[c] + jnp.dot(
                vt_ref[0, hh * LANES:(hh + 1) * LANES, pl.ds(start, tk)], p.astype(BF16),
                preferred_element_type=F32)

    def key_tiles(first, count):
        for d in range(count * per_q):
            tile_update(first * per_q + d, None)

    def body(j, carry):
        key_tiles(2 * j, 2)
        return carry

    lax.fori_loop(0, qi // 2, body, 0)

    @pl.when(qi % 2 == 1)
    def _():
        key_tiles(qi - 1, 1)

    qc = lax.broadcasted_iota(jnp.int32, (tk, tq), 1) // CHUNK
    for d in range(per_q):
        kc = (lax.broadcasted_iota(jnp.int32, (tk, tq), 0) + d * tk) // CHUNK
        tile_update(qi * per_q + d, kc <= qc)

    lam = (jnp.exp(jnp.sum(lq1_ref[...] * lk1_ref[...], axis=1, keepdims=True))
           - jnp.exp(jnp.sum(lq2_ref[...] * lk2_ref[...], axis=1, keepdims=True)) + LAM_INIT)
    for hh in range(N_HEADS):
        c = 2 * hh
        ot = acc_ref[c] / l_ref[c] - lam * (acc_ref[c + 1] / l_ref[c + 1])
        ot = ot * lax.rsqrt(jnp.mean(ot * ot, axis=0, keepdims=True) + LN_EPS)
        o_ref[:, hh * LANES:(hh + 1) * LANES] = (
            ot.T * subg_ref[...] * (1.0 - LAM_INIT)).astype(BF16)


def _attention(nb, seq, q, k, vt, k_meta, vt_meta, lq1, lk1, lq2, lk2, subln_g):
    tq = ATTN_TILE
    nq = seq // tq
    n_maps = 2 * N_HEADS
    vec = lambda b, i: (0, 0)
    return pl.pallas_call(
        _attn_kernel,
        grid=(nb, nq),
        in_specs=[
            pl.BlockSpec((1, HEAD_DIM), vec),
            pl.BlockSpec((1, HEAD_DIM), vec),
            pl.BlockSpec((1, HEAD_DIM), vec),
            pl.BlockSpec((1, HEAD_DIM), vec),
            pl.BlockSpec((1, V_HEAD_DIM), vec),
            pl.BlockSpec((tq, QK_COLS), lambda b, i: (b * nq + i, 0)),
            pl.BlockSpec((seq, QK_COLS), lambda b, i: (b, 0)),
            pl.BlockSpec((1, ATTN_WIDTH, seq), lambda b, i: (b, 0, 0)),
            pl.BlockSpec((META_PAD, QK_COLS), vec),
            pl.BlockSpec((1, ATTN_WIDTH, META_PAD), lambda b, i: (0, 0, 0)),
        ],
        out_specs=pl.BlockSpec((tq, ATTN_WIDTH), lambda b, i: (b * nq + i, 0)),
        out_shape=jax.ShapeDtypeStruct((nb * seq, ATTN_WIDTH), BF16),
        scratch_shapes=[
            pltpu.VMEM((n_maps, LANES, tq), BF16),
            pltpu.VMEM((n_maps, 1, tq), F32),
            pltpu.VMEM((n_maps, 1, tq), F32),
            pltpu.VMEM((n_maps, V_HEAD_DIM, tq), F32),
        ],
        compiler_params=_params(("arbitrary", "arbitrary")),
        name="attn",
    )(lq1, lk1, lq2, lk2, subln_g, q, k, vt, k_meta, vt_meta)


def _conv_kernel(u_ref, prev_ref, um_ref, w_ref, bias_ref, g_ref, b_ref, cu_ref, pad_ref, sh_ref):
    t = pl.program_id(1)
    tc = u_ref.shape[0]
    pad_ref[CONV_HALO:CONV_HALO + tc, :] = u_ref[...]

    @pl.when(t == 0)
    def _():
        pad_ref[0:CONV_HALO - N_META, :] = jnp.zeros((CONV_HALO - N_META, CONV_WIDTH), F32)
        pad_ref[CONV_HALO - N_META:CONV_HALO, :] = um_ref[0:N_META, :]

    @pl.when(t > 0)
    def _():
        pad_ref[0:CONV_HALO, :] = prev_ref[...]

    first = CONV_HALO - (CONV_K - 1)
    span = tc + (first + CONV_K - 1) // SUBLANES * SUBLANES - SUBLANES
    for o in range(1, SUBLANES):
        sh_ref[o - 1] = pad_ref[o:o + span, :]

    for r0 in range(0, tc, CONV_ROWS):
        acc = jnp.zeros((CONV_ROWS, CONV_WIDTH), F32) + bias_ref[...]
        for kk in range(CONV_K):
            o = (first + kk) % SUBLANES
            base = r0 + first + kk - o
            src = pad_ref if o == 0 else sh_ref.at[o - 1]
            acc = acc + src[base:base + CONV_ROWS, :] * w_ref[kk:kk + 1, :]
        y = _layer_norm(acc, g_ref[...], b_ref[...])
        cu_ref[r0:r0 + CONV_ROWS, :] = (y * jax.nn.sigmoid(y)).astype(BF16)


def _conv(nb, seq, u, u_meta, dw_kernel, dw_bias, ln_g, ln_b):
    tc = CONV_TILE
    nt = seq // tc
    per = tc // CONV_HALO
    const = lambda b, t: (0, 0)
    return pl.pallas_call(
        _conv_kernel,
        grid=(nb, nt),
        in_specs=[
            pl.BlockSpec((tc, CONV_WIDTH), lambda b, t: (b * nt + t, 0)),
            pl.BlockSpec((CONV_HALO, CONV_WIDTH),
                         lambda b, t: (jnp.maximum((b * nt + t) * per - 1, 0), 0)),
            pl.BlockSpec((CONV_HALO, CONV_WIDTH), const),
            pl.BlockSpec((CONV_K + 1, CONV_WIDTH), const),
            pl.BlockSpec((1, CONV_WIDTH), const),
            pl.BlockSpec((1, CONV_WIDTH), const),
            pl.BlockSpec((1, CONV_WIDTH), const),
        ],
        out_specs=pl.BlockSpec((tc, CONV_WIDTH), lambda b, t: (b * nt + t, 0)),
        out_shape=jax.ShapeDtypeStruct((nb * seq, CONV_WIDTH), BF16),
        scratch_shapes=[
            pltpu.VMEM((CONV_HALO + tc, CONV_WIDTH), F32),
            pltpu.VMEM((SUBLANES - 1, CONV_HALO + tc - SUBLANES, CONV_WIDTH), F32),
        ],
        compiler_params=_params(("arbitrary", "arbitrary")),
        name="conv",
    )(u, u, u_meta, dw_kernel, dw_bias, ln_g, ln_b)


def _merge_kernel(x_ref, o_ref, cu_ref, sg_ref, lg_ref, lb_ref, wa_ref, wc_ref, bc_ref,
                  wo_ref, bo_ref, g1_ref, b1_ref, h1_ref):
    h = _layer_norm(x_ref[...], lg_ref[...], lb_ref[...])
    attn_d = jnp.dot(o_ref[...], wa_ref[...], preferred_element_type=F32)
    conv_d = jnp.dot(cu_ref[...], wc_ref[...], preferred_element_type=F32) + bc_ref[...]
    z = sg_ref[:, 0:D_MODEL] * attn_d + sg_ref[:, D_MODEL:2 * D_MODEL] * conv_d
    y = jnp.dot(z.astype(BF16), wo_ref[...], preferred_element_type=F32) + bo_ref[...]
    h1_ref[...] = _layer_norm(DEEPNORM_ALPHA * h + y, g1_ref[...], b1_ref[...])


def _merge(rows, x2, o, cu, sg, ln_g, ln_b, wa, wc, bc, wo, bo, g1, b1):
    tile = MERGE_TILE
    row = lambda i: (i, 0)
    const = lambda i: (0, 0)
    vec = pl.BlockSpec((1, D_MODEL), const)
    mat = pl.BlockSpec((D_MODEL, D_MODEL), const)
    return pl.pallas_call(
        _merge_kernel,
        grid=(rows // tile,),
        in_specs=[
            pl.BlockSpec((tile, D_MODEL), row),
            pl.BlockSpec((tile, ATTN_WIDTH), row),
            pl.BlockSpec((tile, CONV_WIDTH), row),
            pl.BlockSpec((tile, 2 * D_MODEL), row),
            vec, vec, mat, mat, vec, mat, vec, vec, vec,
        ],
        out_specs=pl.BlockSpec((tile, D_MODEL), row),
        out_shape=jax.ShapeDtypeStruct((rows, D_MODEL), F32),
        compiler_params=_params(("arbitrary",)),
        name="merge",
    )(x2, o, cu, sg, ln_g, ln_b, wa, wc, bc, wo, bo, g1, b1)


def _candidate_pairs():
    return [(a, b) for a in range(PEER_TOPK) for b in range(PEER_TOPK // (a + 1))]


def _sorting_network(n):
    pairs = []
    p = 1
    while p < n:
        k = p
        while k >= 1:
            for j in range(k % p, n - k, 2 * k):
                for i in range(min(k, n - j - k)):
                    if (i + j) // (2 * p) == (i + j + k) // (2 * p):
                        pairs.append((i + j, i + j + k))
            k //= 2
        p *= 2
    return pairs


def _top_values(s, count):
    groups = s.shape[0] // SUBLANES
    assert groups & (groups - 1) == 0
    cols = [s[g * SUBLANES:(g + 1) * SUBLANES, :] for g in range(groups)]
    for i, j in _sorting_network(groups):
        hi, lo = jnp.maximum(cols[i], cols[j]), jnp.minimum(cols[i], cols[j])
        cols[i], cols[j] = hi, lo
    vals = []
    for r in range(count):
        mx = jnp.max(cols[0], axis=0, keepdims=True)
        vals.append(mx)
        hit = cols[0] == mx
        need = count - 1 - r
        for k in range(min(groups - 1, need)):
            cols[k] = jnp.where(hit, cols[k + 1], cols[k])
        if need >= groups:
            cols[groups - 1] = jnp.where(hit, -jnp.inf, cols[groups - 1])
    return vals


def _paired_bf16_words(x):
    bits = pltpu.bitcast(x.astype(BF16).astype(F32), jnp.uint32)
    return bits | lax.shift_right_logical(bits, jnp.uint32(16))


def _route_kernel(h_ref, wqt_ref, sk_ref, r2_ref, e2_ref, n1_ref, w1_ref):
    hb = h_ref[...].astype(BF16)
    nt = (((1,), (1,)), ((), ()))
    qt = lax.dot_general(wqt_ref[...], hb, nt, preferred_element_type=F32).astype(BF16)
    half = PEER_DKEY // 2
    tok = hb.shape[0]
    for hh in range(PEER_HEADS):
        s1 = jnp.dot(sk_ref[2 * hh], qt[2 * hh * half:(2 * hh + 1) * half, :],
                     preferred_element_type=F32)
        s2 = jnp.dot(sk_ref[2 * hh + 1], qt[(2 * hh + 1) * half:(2 * hh + 2) * half, :],
                     preferred_element_type=F32)
        top1 = _top_values(s1, PEER_TOPK)
        top2 = _top_values(s2, PEER_TOPK)
        rank2 = jnp.zeros_like(s2)
        for r in range(PEER_TOPK):
            rank2 = rank2 + jnp.where(s2 < top2[r], 1.0, 0.0)
        pairs = _candidate_pairs()
        cands = [top1[a] + top2[b] for a, b in pairs]
        pad = CAND_ROWS - len(cands)
        cand = jnp.concatenate(cands + [jnp.full((pad, tok), -jnp.inf, F32)], axis=0)
        best = _top_values(cand, PEER_TOPK)
        tau = best[PEER_TOPK - 1]
        z = jnp.ones_like(tau)
        for r in range(1, PEER_TOPK):
            z = z + jnp.exp(best[r] - best[0])
        n1 = jnp.zeros_like(s1)
        for a in range(PEER_TOPK):
            count = jnp.zeros_like(tau)
            for (pa, _), c in zip(pairs, cands):
                if pa == a:
                    count = count + jnp.where(c >= tau, 1.0, 0.0)
            n1 = jnp.where(s1 == top1[a], count, n1)
        r2_ref[0, hh] = rank2.astype(BF16)
        e2_ref[0, hh] = jnp.exp(s2 - top2[0]).astype(BF16)
        n1_ref[0, hh] = _paired_bf16_words(n1)
        w1_ref[0, hh] = _paired_bf16_words(jnp.exp(s1 - top1[0]) * (1.0 / z))


def _route(rows, h1, wqt, sk):
    tile = ROUTE_TILE
    blk = pl.BlockSpec((1, PEER_HEADS, PEER_KEYS, tile), lambda i: (i, 0, 0, 0))
    shape = (rows // tile, PEER_HEADS, PEER_KEYS, tile)
    return pl.pallas_call(
        _route_kernel,
        grid=(rows // tile,),
        in_specs=[
            pl.BlockSpec((tile, D_MODEL), lambda i: (i, 0)),
            pl.BlockSpec((PEER_HEADS * PEER_DKEY, D_MODEL), lambda i: (0, 0)),
            pl.BlockSpec((2 * PEER_HEADS, PEER_KEYS, PEER_DKEY // 2), lambda i: (0, 0, 0)),
        ],
        out_specs=[blk, blk, blk, blk],
        out_shape=[
            jax.ShapeDtypeStruct(shape, BF16),
            jax.ShapeDtypeStruct(shape, BF16),
            jax.ShapeDtypeStruct(shape, jnp.uint32),
            jax.ShapeDtypeStruct(shape, jnp.uint32),
        ],
        compiler_params=_params(("arbitrary",)),
        name="route",
    )(h1, wqt, sk)


def _peer_kernel(h_ref, u_ref, vt_ref, r2_ref, e2_ref, n1_ref, w1_ref, g2_ref, b2_ref, out_ref,
                 ht_ref, acc_ref, coef_ref):
    e = pl.program_id(1)
    ne = pl.num_programs(1)
    tok = h_ref.shape[0]
    n_first = PEER_EXP // PEER_KEYS

    @pl.when(e == 0)
    def _():
        ht_ref[...] = h_ref[...].T.astype(BF16)
        acc_ref[...] = jnp.zeros_like(acc_ref)

    row0 = pl.multiple_of(e * n_first, n_first)
    n_rows = [n1_ref[0, hh, pl.ds(row0, n_first), :] for hh in range(PEER_HEADS)]
    w_rows = [w1_ref[0, hh, pl.ds(row0, n_first), :] for hh in range(PEER_HEADS)]

    def packed_row(rows, ii):
        word_tile = jnp.broadcast_to(rows[ii:ii + 1, :], (SUBLANES, tok))
        return pltpu.bitcast(word_tile, BF16)

    for ii in range(n_first):
        act = jnp.dot(u_ref[ii * PEER_KEYS:(ii + 1) * PEER_KEYS, :], ht_ref[...],
                      preferred_element_type=F32)
        act = (0.5 * act * (1.0 + lax.erf(act * (2.0 ** -0.5)))).astype(BF16)
        n_b = [packed_row(n_rows[hh], ii) for hh in range(PEER_HEADS)]
        w_b = [packed_row(w_rows[hh], ii) for hh in range(PEER_HEADS)]
        for c in range(PEER_KEYS // BF16_ROWS):
            js = slice(c * BF16_ROWS, (c + 1) * BF16_ROWS)
            gate = jnp.zeros((BF16_ROWS, tok), BF16)
            for hh in range(PEER_HEADS):
                e2 = e2_ref[0, hh, js, :]
                gate = gate + jnp.where(r2_ref[0, hh, js, :] < n_b[hh], e2 * w_b[hh],
                                        jnp.zeros_like(e2))
            r = ii * PEER_KEYS + c * BF16_ROWS
            coef_ref[r:r + BF16_ROWS, :] = gate * act[js, :]
    acc_ref[...] += jnp.dot(vt_ref[0], coef_ref[...], preferred_element_type=F32)

    @pl.when(e == ne - 1)
    def _():
        y = DEEPNORM_ALPHA * h_ref[...] + acc_ref[...].T
        out_ref[...] = _layer_norm(y, g2_ref[...], b2_ref[...])


def _peer(rows, h1, u_bf, vt_bf, r2, e2, n1, w1, g2, b2):
    tok = PEER_TOK
    sel = pl.BlockSpec((1, PEER_HEADS, PEER_KEYS, tok), lambda t, e: (t, 0, 0, 0))
    return pl.pallas_call(
        _peer_kernel,
        grid=(rows // tok, PEER_N // PEER_EXP),
        in_specs=[
            pl.BlockSpec((tok, D_MODEL), lambda t, e: (t, 0)),
            pl.BlockSpec((PEER_EXP, D_MODEL), lambda t, e: (e, 0)),
            pl.BlockSpec((1, D_MODEL, PEER_EXP), lambda t, e: (e, 0, 0)),
            sel, sel, sel, sel,
            pl.BlockSpec((1, D_MODEL), lambda t, e: (0, 0)),
            pl.BlockSpec((1, D_MODEL), lambda t, e: (0, 0)),
        ],
        out_specs=pl.BlockSpec((tok, D_MODEL), lambda t, e: (t, 0)),
        out_shape=jax.ShapeDtypeStruct((rows, D_MODEL), F32),
        scratch_shapes=[
            pltpu.VMEM((D_MODEL, tok), BF16),
            pltpu.VMEM((D_MODEL, tok), F32),
            pltpu.VMEM((PEER_EXP, tok), BF16),
        ],
        compiler_params=_params(("arbitrary", "arbitrary")),
        name="peer",
    )(h1, u_bf, vt_bf, r2, e2, n1, w1, g2, b2)


def kernel(x, meta, ln_in_g, ln_in_b, w_in, b_in, lambda_q1, lambda_k1, lambda_q2, lambda_k2,
           subln_g, w_attn_proj, dw_kernel, dw_bias, conv_ln_g, conv_ln_b, w_conv_proj,
           b_conv_proj, w_out, b_out, ln1_g, ln1_b, peer_wq, peer_subkeys, peer_u, peer_v,
           ln2_g, ln2_b):
    nb, seq, _ = x.shape
    rows = nb * seq
    assert w_in.shape[0] == DEPTH and seq % PEER_TOK == 0 and PEER_TOK == ROUTE_TILE
    vec = lambda a: a.reshape(1, -1)
    x2 = x.reshape(rows, D_MODEL)
    w_bf = w_in[0].astype(BF16)
    bias = vec(b_in[0])
    g_in, b_ln = vec(ln_in_g), vec(ln_in_b)

    cos_t, sa_t, sb_t = _rope_tables(N_META + seq)
    frame_tabs = [t[N_META:] for t in (cos_t, sa_t, sb_t)]
    meta_tabs = [t[:META_PAD] for t in (cos_t, sa_t, sb_t)]
    meta_rows = jnp.pad(meta, ((0, META_PAD - N_META), (0, 0)))

    q, k, vt, u, sg = _in_proj(rows, seq, IN_PROJ_TILE, x2, g_in, b_ln, w_bf, bias, *frame_tabs)
    _, k_meta, vt_meta, u_meta, _ = _in_proj(META_PAD, META_PAD, META_PAD, meta_rows, g_in, b_ln,
                                             w_bf, bias, *meta_tabs)

    o = _attention(nb, seq, q, k, vt, k_meta, vt_meta, vec(lambda_q1[0]), vec(lambda_k1[0]),
                   vec(lambda_q2[0]), vec(lambda_k2[0]), vec(subln_g[0]))
    dw = jnp.pad(dw_kernel[0], ((0, 1), (0, 0)))
    cu = _conv(nb, seq, u, u_meta, dw, vec(dw_bias[0]), vec(conv_ln_g[0]), vec(conv_ln_b[0]))
    h1 = _merge(rows, x2, o, cu, sg, g_in, b_ln, w_attn_proj[0].astype(BF16),
                w_conv_proj[0].astype(BF16), vec(b_conv_proj[0]), w_out[0].astype(BF16),
                vec(b_out[0]), vec(ln1_g[0]), vec(ln1_b[0]))

    wqt = peer_wq[0].T.astype(BF16)
    sk = peer_subkeys[0].reshape(2 * PEER_HEADS, PEER_KEYS, PEER_DKEY // 2).astype(BF16)
    r2, e2, n1, w1 = _route(rows, h1, wqt, sk)
    vt_blocks = (peer_v[0].astype(BF16).reshape(PEER_N // PEER_EXP, PEER_EXP, D_MODEL)
                 .transpose(0, 2, 1))
    h2 = _peer(rows, h1, peer_u[0].astype(BF16), vt_blocks, r2, e2, n1, w1,
               vec(ln2_g[0]), vec(ln2_b[0]))
    return h2.reshape(nb, seq, D_MODEL)
```

```python
import math

import jax
import jax.numpy as jnp
from jax import lax
from jax.experimental import pallas as pl
from jax.experimental.pallas import tpu as pltpu

D_MODEL = 1024
CHUNK = 64
N_META = 16
N_HEADS = 8
HEAD_DIM = 64
V_HEAD_DIM = 2 * HEAD_DIM
QK_COLS = N_HEADS * 2 * HEAD_DIM
ATTN_WIDTH = N_HEADS * V_HEAD_DIM
ROPE_DIMS = HEAD_DIM // 4
ROPE_THETA = 500000.0
NEG_INF = -1e30
CONV_WIDTH = D_MODEL
CONV_K = 31
PEER_HEADS = 8
PEER_KEYS = 128
PEER_N = PEER_KEYS * PEER_KEYS
PEER_DKEY = 256
PEER_TOPK = 16
LN_EPS = 1e-5
DEPTH = 1
DEEPNORM_ALPHA = (2 * DEPTH) ** 0.25
LAM_INIT = 0.8 - 0.6 * math.exp(-0.3 * 0)
LOG2_E = math.log2(math.e)

LANES = 128
SUBLANES = 8
CONV_ROWS = 128
BF16_ROWS = 16
CAND_ROWS = 64
META_PAD = 128
V7X_VMEM_BYTES = 64 * 1024 * 1024
VMEM_LIMIT = V7X_VMEM_BYTES * 7 // 8

IN_PROJ_TILE = 256
MERGE_TILE = 512
ATTN_TILE = 256
ATTN_KEYS = 128
CONV_TILE = 512
CONV_HALO = 32
ROUTE_TILE = 512
PEER_TOK = 512
PEER_EXP = 2048

F32 = jnp.float32
BF16 = jnp.bfloat16


def _layer_norm(x, g, b):
    mu = jnp.mean(x, axis=-1, keepdims=True)
    xc = x - mu
    var = jnp.mean(xc * xc, axis=-1, keepdims=True)
    return xc * lax.rsqrt(var + LN_EPS) * g + b


def _params(sem, vmem=VMEM_LIMIT):
    return pltpu.CompilerParams(dimension_semantics=sem, vmem_limit_bytes=vmem)


def _in_proj_kernel(x_ref, g_ref, b_ref, w_ref, bias_ref, cos_ref, sa_ref, sb_ref,
                    q_ref, k_ref, vt_ref, u_ref, sg_ref):
    hb = _layer_norm(x_ref[...], g_ref[...], b_ref[...]).astype(BF16)

    def proj(c0, n):
        return (jnp.dot(hb, w_ref[:, c0:c0 + n], preferred_element_type=F32)
                + bias_ref[:, c0:c0 + n])

    cos, sa, sb = cos_ref[...], sa_ref[...], sb_ref[...]

    def rope_store(t, out_ref, scale):
        for hh in range(QK_COLS // LANES):
            th = t[:, hh * LANES:(hh + 1) * LANES]
            r = (th * cos + pltpu.roll(th, LANES - ROPE_DIMS // 2, 1) * sa
                 + pltpu.roll(th, ROPE_DIMS // 2, 1) * sb)
            out_ref[:, hh * LANES:(hh + 1) * LANES] = (r * scale).astype(BF16)

    rope_store(proj(0, QK_COLS), q_ref, HEAD_DIM ** -0.5 * LOG2_E)
    rope_store(proj(QK_COLS, QK_COLS), k_ref, 1.0)
    vt_ref[0] = proj(2 * QK_COLS, ATTN_WIDTH).T.astype(BF16)
    c0 = 2 * QK_COLS + ATTN_WIDTH
    a = proj(c0, CONV_WIDTH)
    gt = proj(c0 + CONV_WIDTH, CONV_WIDTH)
    u_ref[...] = a * jax.nn.sigmoid(gt)
    sg_ref[...] = jax.nn.sigmoid(proj(c0 + 2 * CONV_WIDTH, 2 * D_MODEL))


def _in_proj(rows, seq, tile, x2, ln_g, ln_b, w_bf, bias, cos_t, sa_t, sb_t):
    n_seq = seq // tile
    nb = rows // seq
    in_cols = w_bf.shape[1]
    row = lambda i: (i, 0)
    const = lambda i: (0, 0)
    tab = lambda i: (i % n_seq, 0)
    return pl.pallas_call(
        _in_proj_kernel,
        grid=(rows // tile,),
        in_specs=[
            pl.BlockSpec((tile, D_MODEL), row),
            pl.BlockSpec((1, D_MODEL), const),
            pl.BlockSpec((1, D_MODEL), const),
            pl.BlockSpec((D_MODEL, in_cols), const, pipeline_mode=pl.Buffered(1)),
            pl.BlockSpec((1, in_cols), const),
            pl.BlockSpec((tile, LANES), tab),
            pl.BlockSpec((tile, LANES), tab),
            pl.BlockSpec((tile, LANES), tab),
        ],
        out_specs=[
            pl.BlockSpec((tile, QK_COLS), row),
            pl.BlockSpec((tile, QK_COLS), row),
            pl.BlockSpec((1, ATTN_WIDTH, tile), lambda i: (i // n_seq, 0, i % n_seq)),
            pl.BlockSpec((tile, CONV_WIDTH), row),
            pl.BlockSpec((tile, 2 * D_MODEL), row),
        ],
        out_shape=[
            jax.ShapeDtypeStruct((rows, QK_COLS), BF16),
            jax.ShapeDtypeStruct((rows, QK_COLS), BF16),
            jax.ShapeDtypeStruct((nb, ATTN_WIDTH, seq), BF16),
            jax.ShapeDtypeStruct((rows, CONV_WIDTH), F32),
            jax.ShapeDtypeStruct((rows, 2 * D_MODEL), F32),
        ],
        compiler_params=_params(("arbitrary",)),
        name="in_proj",
    )(x2, ln_g, ln_b, w_bf, bias, cos_t, sa_t, sb_t)


def _rope_tables(length):
    pos = jnp.arange(length, dtype=F32)
    inv = ROPE_THETA ** (-jnp.arange(0, ROPE_DIMS, 2, dtype=F32) / ROPE_DIMS)
    ang = pos[:, None] * inv[None, :]
    cos, sin = jnp.cos(ang), jnp.sin(ang)
    half = ROPE_DIMS // 2
    ones = jnp.ones((length, HEAD_DIM - ROPE_DIMS), F32)
    zeros = jnp.zeros((length, HEAD_DIM - ROPE_DIMS), F32)
    zh = jnp.zeros((length, half), F32)
    cos_h = jnp.concatenate([cos, cos, ones], axis=1)
    sa_h = jnp.concatenate([-sin, zh, zeros], axis=1)
    sb_h = jnp.concatenate([zh, sin, zeros], axis=1)
    rep = LANES // HEAD_DIM
    return (jnp.tile(cos_h, (1, rep)), jnp.tile(sa_h, (1, rep)), jnp.tile(sb_h, (1, rep)))


def _attn_kernel(lq1_ref, lk1_ref, lq2_ref, lk2_ref, subg_ref, q_ref, k_ref, vt_ref, km_ref, vtm_ref,
                 o_ref, qm_ref, m_ref, l_ref, acc_ref):
    qi = pl.program_id(1)
    tq = q_ref.shape[0]
    dim = lax.broadcasted_iota(jnp.int32, (LANES, tq), 0)
    maps = [(hh, m) for hh in range(N_HEADS) for m in range(2)]

    for hh in range(N_HEADS):
        qt = q_ref[:, hh * LANES:(hh + 1) * LANES].astype(F32).T
        qm_ref[2 * hh] = jnp.where(dim < HEAD_DIM, qt, 0.0).astype(BF16)
        qm_ref[2 * hh + 1] = jnp.where(dim >= HEAD_DIM, qt, 0.0).astype(BF16)

    def scores(keys, c):
        return jnp.dot(keys, qm_ref[c], preferred_element_type=F32)

    tk = ATTN_KEYS
    per_q = tq // tk

    def softmax_step(c, s, vt_keys, first):
        mx = jnp.max(s, axis=0, keepdims=True)
        m_new = mx if first else jnp.maximum(m_ref[c], mx)
        p = jnp.exp2(s - m_new)
        pv = jnp.dot(vt_keys, p.astype(BF16), preferred_element_type=F32)
        if first:
            l_ref[c] = jnp.sum(p, axis=0, keepdims=True)
            acc_ref[c] = pv
        else:
            alpha = jnp.exp2(m_ref[c] - m_new)
            l_ref[c] = alpha * l_ref[c] + jnp.sum(p, axis=0, keepdims=True)
            acc_ref[c] = alpha * acc_ref[c] + pv
        m_ref[c] = m_new

    def tile_update(j, mask, first=False):
        start = pl.multiple_of(j * tk, tk)
        for hh, m in maps:
            c = 2 * hh + m
            hs = slice(hh * LANES, (hh + 1) * LANES)
            s = scores(k_ref[pl.ds(start, tk), hs], c)
            if mask is not None:
                s = jnp.where(mask, s, NEG_INF)
            softmax_step(c, s, vt_ref[0, hs, pl.ds(start, tk)], first)

    qc = lax.broadcasted_iota(jnp.int32, (tk, tq), 1) // CHUNK
    for d in range(per_q):
        kc = (lax.broadcasted_iota(jnp.int32, (tk, tq), 0) + d * tk) // CHUNK
        tile_update(qi * per_q + d, kc <= qc, first=(d == 0))

    meta_ok = lax.broadcasted_iota(jnp.int32, (META_PAD, tq), 0) < N_META
    for hh, m in maps:
        c = 2 * hh + m
        hs = slice(hh * LANES, (hh + 1) * LANES)
        s = jnp.where(meta_ok, scores(km_ref[:, hs], c), NEG_INF)
        softmax_step(c, s, vtm_ref[0, hs, :], False)

    def key_tiles(first, count):
        for d in range(count * per_q):
            tile_update(first * per_q + d, None)

    def body(j, carry):
        key_tiles(2 * j, 2)
        return carry

    lax.fori_loop(0, qi // 2, body, 0)

    @pl.when(qi % 2 == 1)
    def _():
        key_tiles(qi - 1, 1)


    lam = (jnp.exp(jnp.sum(lq1_ref[...] * lk1_ref[...], axis=1, keepdims=True))
           - jnp.exp(jnp.sum(lq2_ref[...] * lk2_ref[...], axis=1, keepdims=True)) + LAM_INIT)
    for hh in range(N_HEADS):
        c = 2 * hh
        ot = acc_ref[c] / l_ref[c] - lam * (acc_ref[c + 1] / l_ref[c + 1])
        ot = ot * lax.rsqrt(jnp.mean(ot * ot, axis=0, keepdims=True) + LN_EPS)
        o_ref[:, hh * LANES:(hh + 1) * LANES] = (
            ot.T * subg_ref[...] * (1.0 - LAM_INIT)).astype(BF16)


def _attention(nb, seq, q, k, vt, k_meta, vt_meta, lq1, lk1, lq2, lk2, subln_g):
    tq = ATTN_TILE
    nq = seq // tq
    n_maps = 2 * N_HEADS
    vec = lambda b, i: (0, 0)
    return pl.pallas_call(
        _attn_kernel,
        grid=(nb, nq),
        in_specs=[
            pl.BlockSpec((1, HEAD_DIM), vec),
            pl.BlockSpec((1, HEAD_DIM), vec),
            pl.BlockSpec((1, HEAD_DIM), vec),
            pl.BlockSpec((1, HEAD_DIM), vec),
            pl.BlockSpec((1, V_HEAD_DIM), vec),
            pl.BlockSpec((tq, QK_COLS), lambda b, i: (b * nq + i, 0)),
            pl.BlockSpec((seq, QK_COLS), lambda b, i: (b, 0)),
            pl.BlockSpec((1, ATTN_WIDTH, seq), lambda b, i: (b, 0, 0)),
            pl.BlockSpec((META_PAD, QK_COLS), vec),
            pl.BlockSpec((1, ATTN_WIDTH, META_PAD), lambda b, i: (0, 0, 0)),
        ],
        out_specs=pl.BlockSpec((tq, ATTN_WIDTH), lambda b, i: (b * nq + i, 0)),
        out_shape=jax.ShapeDtypeStruct((nb * seq, ATTN_WIDTH), BF16),
        scratch_shapes=[
            pltpu.VMEM((n_maps, LANES, tq), BF16),
            pltpu.VMEM((n_maps, 1, tq), F32),
            pltpu.VMEM((n_maps, 1, tq), F32),
            pltpu.VMEM((n_maps, V_HEAD_DIM, tq), F32),
        ],
        compiler_params=_params(("arbitrary", "arbitrary")),
        name="attn",
    )(lq1, lk1, lq2, lk2, subln_g, q, k, vt, k_meta, vt_meta)


def _conv_kernel(u_ref, prev_ref, um_ref, w_ref, bias_ref, g_ref, b_ref, cu_ref, pad_ref, sh_ref):
    t = pl.program_id(1)
    tc = u_ref.shape[0]
    pad_ref[CONV_HALO:CONV_HALO + tc, :] = u_ref[...]

    @pl.when(t == 0)
    def _():
        pad_ref[0:CONV_HALO - N_META, :] = jnp.zeros((CONV_HALO - N_META, CONV_WIDTH), F32)
        pad_ref[CONV_HALO - N_META:CONV_HALO, :] = um_ref[0:N_META, :]

    @pl.when(t > 0)
    def _():
        pad_ref[0:CONV_HALO, :] = prev_ref[...]

    first = CONV_HALO - (CONV_K - 1)
    span = tc + (first + CONV_K - 1) // SUBLANES * SUBLANES - SUBLANES
    for o in range(1, SUBLANES):
        sh_ref[o - 1] = pad_ref[o:o + span, :]

    for r0 in range(0, tc, CONV_ROWS):
        acc = jnp.zeros((CONV_ROWS, CONV_WIDTH), F32) + bias_ref[...]
        for kk in range(CONV_K):
            o = (first + kk) % SUBLANES
            base = r0 + first + kk - o
            src = pad_ref if o == 0 else sh_ref.at[o - 1]
            acc = acc + src[base:base + CONV_ROWS, :] * w_ref[kk:kk + 1, :]
        y = _layer_norm(acc, g_ref[...], b_ref[...])
        cu_ref[r0:r0 + CONV_ROWS, :] = (y * jax.nn.sigmoid(y)).astype(BF16)


def _conv(nb, seq, u, u_meta, dw_kernel, dw_bias, ln_g, ln_b):
    tc = CONV_TILE
    nt = seq // tc
    per = tc // CONV_HALO
    const = lambda b, t: (0, 0)
    return pl.pallas_call(
        _conv_kernel,
        grid=(nb, nt),
        in_specs=[
            pl.BlockSpec((tc, CONV_WIDTH), lambda b, t: (b * nt + t, 0)),
            pl.BlockSpec((CONV_HALO, CONV_WIDTH),
                         lambda b, t: (jnp.maximum((b * nt + t) * per - 1, 0), 0)),
            pl.BlockSpec((CONV_HALO, CONV_WIDTH), const),
            pl.BlockSpec((CONV_K + 1, CONV_WIDTH), const),
            pl.BlockSpec((1, CONV_WIDTH), const),
            pl.BlockSpec((1, CONV_WIDTH), const),
            pl.BlockSpec((1, CONV_WIDTH), const),
        ],
        out_specs=pl.BlockSpec((tc, CONV_WIDTH), lambda b, t: (b * nt + t, 0)),
        out_shape=jax.ShapeDtypeStruct((nb * seq, CONV_WIDTH), BF16),
        scratch_shapes=[
            pltpu.VMEM((CONV_HALO + tc, CONV_WIDTH), F32),
            pltpu.VMEM((SUBLANES - 1, CONV_HALO + tc - SUBLANES, CONV_WIDTH), F32),
        ],
        compiler_params=_params(("arbitrary", "arbitrary")),
        name="conv",
    )(u, u, u_meta, dw_kernel, dw_bias, ln_g, ln_b)


def _merge_kernel(x_ref, o_ref, cu_ref, sg_ref, lg_ref, lb_ref, wa_ref, wc_ref, bc_ref,
                  wo_ref, bo_ref, g1_ref, b1_ref, h1_ref):
    h = _layer_norm(x_ref[...], lg_ref[...], lb_ref[...])
    attn_d = jnp.dot(o_ref[...], wa_ref[...], preferred_element_type=F32)
    conv_d = jnp.dot(cu_ref[...], wc_ref[...], preferred_element_type=F32) + bc_ref[...]
    z = sg_ref[:, 0:D_MODEL] * attn_d + sg_ref[:, D_MODEL:2 * D_MODEL] * conv_d
    y = jnp.dot(z.astype(BF16), wo_ref[...], preferred_element_type=F32) + bo_ref[...]
    h1_ref[...] = _layer_norm(DEEPNORM_ALPHA * h + y, g1_ref[...], b1_ref[...])


def _merge(rows, x2, o, cu, sg, ln_g, ln_b, wa, wc, bc, wo, bo, g1, b1):
    tile = MERGE_TILE
    row = lambda i: (i, 0)
    const = lambda i: (0, 0)
    vec = pl.BlockSpec((1, D_MODEL), const)
    mat = pl.BlockSpec((D_MODEL, D_MODEL), const)
    return pl.pallas_call(
        _merge_kernel,
        grid=(rows // tile,),
        in_specs=[
            pl.BlockSpec((tile, D_MODEL), row),
            pl.BlockSpec((tile, ATTN_WIDTH), row),
            pl.BlockSpec((tile, CONV_WIDTH), row),
            pl.BlockSpec((tile, 2 * D_MODEL), row),
            vec, vec, mat, mat, vec, mat, vec, vec, vec,
        ],
        out_specs=pl.BlockSpec((tile, D_MODEL), row),
        out_shape=jax.ShapeDtypeStruct((rows, D_MODEL), F32),
        compiler_params=_params(("arbitrary",)),
        name="merge",
    )(x2, o, cu, sg, ln_g, ln_b, wa, wc, bc, wo, bo, g1, b1)


def _candidate_pairs():
    return [(a, b) for a in range(PEER_TOPK) for b in range(PEER_TOPK // (a + 1))]


def _sorting_network(n):
    pairs = []
    p = 1
    while p < n:
        k = p
        while k >= 1:
            for j in range(k % p, n - k, 2 * k):
                for i in range(min(k, n - j - k)):
                    if (i + j) // (2 * p) == (i + j + k) // (2 * p):
                        pairs.append((i + j, i + j + k))
            k //= 2
        p *= 2
    return pairs


def _top_values(s, count):
    groups = s.shape[0] // SUBLANES
    assert groups & (groups - 1) == 0
    cols = [s[g * SUBLANES:(g + 1) * SUBLANES, :] for g in range(groups)]
    for i, j in _sorting_network(groups):
        hi, lo = jnp.maximum(cols[i], cols[j]), jnp.minimum(cols[i], cols[j])
        cols[i], cols[j] = hi, lo
    vals = []
    for r in range(count):
        mx = jnp.max(cols[0], axis=0, keepdims=True)
        vals.append(mx)
        hit = cols[0] == mx
        need = count - 1 - r
        for k in range(min(groups - 1, need)):
            cols[k] = jnp.where(hit, cols[k + 1], cols[k])
        if need >= groups:
            cols[groups - 1] = jnp.where(hit, -jnp.inf, cols[groups - 1])
    return vals


def _paired_bf16_words(x):
    bits = pltpu.bitcast(x.astype(BF16).astype(F32), jnp.uint32)
    return bits | lax.shift_right_logical(bits, jnp.uint32(16))


def _route_kernel(h_ref, wqt_ref, sk_ref, r2_ref, e2_ref, n1_ref, w1_ref):
    hb = h_ref[...].astype(BF16)
    nt = (((1,), (1,)), ((), ()))
    qt = lax.dot_general(wqt_ref[...], hb, nt, preferred_element_type=F32).astype(BF16)
    half = PEER_DKEY // 2
    tok = hb.shape[0]
    for hh in range(PEER_HEADS):
        s1 = jnp.dot(sk_ref[2 * hh], qt[2 * hh * half:(2 * hh + 1) * half, :],
                     preferred_element_type=F32)
        s2 = jnp.dot(sk_ref[2 * hh + 1], qt[(2 * hh + 1) * half:(2 * hh + 2) * half, :],
                     preferred_element_type=F32)
        top1 = _top_values(s1, PEER_TOPK)
        top2 = _top_values(s2, PEER_TOPK)
        rank2 = jnp.zeros_like(s2)
        for r in range(PEER_TOPK):
            rank2 = rank2 + jnp.where(s2 < top2[r], 1.0, 0.0)
        pairs = _candidate_pairs()
        cands = [top1[a] + top2[b] for a, b in pairs]
        pad = CAND_ROWS - len(cands)
        cand = jnp.concatenate(cands + [jnp.full((pad, tok), -jnp.inf, F32)], axis=0)
        best = _top_values(cand, PEER_TOPK)
        tau = best[PEER_TOPK - 1]
        z = jnp.ones_like(tau)
        for r in range(1, PEER_TOPK):
            z = z + jnp.exp(best[r] - best[0])
        n1 = jnp.zeros_like(s1)
        for a in range(PEER_TOPK):
            count = jnp.zeros_like(tau)
            for (pa, _), c in zip(pairs, cands):
                if pa == a:
                    count = count + jnp.where(c >= tau, 1.0, 0.0)
            n1 = jnp.where(s1 == top1[a], count, n1)
        r2_ref[0, hh] = rank2.astype(BF16)
        e2_ref[0, hh] = jnp.exp(s2 - top2[0]).astype(BF16)
        n1_ref[0, hh] = _paired_bf16_words(n1)
        w1_ref[0, hh] = _paired_bf16_words(jnp.exp(s1 - top1[0]) * (1.0 / z))


def _route(rows, h1, wqt, sk):
    tile = ROUTE_TILE
    blk = pl.BlockSpec((1, PEER_HEADS, PEER_KEYS, tile), lambda i: (i, 0, 0, 0))
    shape = (rows // tile, PEER_HEADS, PEER_KEYS, tile)
    return pl.pallas_call(
        _route_kernel,
        grid=(rows // tile,),
        in_specs=[
            pl.BlockSpec((tile, D_MODEL), lambda i: (i, 0)),
            pl.BlockSpec((PEER_HEADS * PEER_DKEY, D_MODEL), lambda i: (0, 0)),
            pl.BlockSpec((2 * PEER_HEADS, PEER_KEYS, PEER_DKEY // 2), lambda i: (0, 0, 0)),
        ],
        out_specs=[blk, blk, blk, blk],
        out_shape=[
            jax.ShapeDtypeStruct(shape, BF16),
            jax.ShapeDtypeStruct(shape, BF16),
            jax.ShapeDtypeStruct(shape, jnp.uint32),
            jax.ShapeDtypeStruct(shape, jnp.uint32),
        ],
        compiler_params=_params(("arbitrary",)),
        name="route",
    )(h1, wqt, sk)


def _peer_kernel(h_ref, u_ref, vt_ref, r2_ref, e2_ref, n1_ref, w1_ref, g2_ref, b2_ref, out_ref,
                 ht_ref, acc_ref, coef_ref):
    e = pl.program_id(1)
    ne = pl.num_programs(1)
    tok = h_ref.shape[0]
    n_first = PEER_EXP // PEER_KEYS

    @pl.when(e == 0)
    def _():
        ht_ref[...] = h_ref[...].T.astype(BF16)
        acc_ref[...] = jnp.zeros_like(acc_ref)

    row0 = pl.multiple_of(e * n_first, n_first)
    n_rows = [n1_ref[0, hh, pl.ds(row0, n_first), :] for hh in range(PEER_HEADS)]
    w_rows = [w1_ref[0, hh, pl.ds(row0, n_first), :] for hh in range(PEER_HEADS)]

    def packed_row(rows, ii):
        word_tile = jnp.broadcast_to(rows[ii:ii + 1, :], (SUBLANES, tok))
        return pltpu.bitcast(word_tile, BF16)

    for ii in range(n_first):
        act = jnp.dot(u_ref[ii * PEER_KEYS:(ii + 1) * PEER_KEYS, :], ht_ref[...],
                      preferred_element_type=F32)
        act = (0.5 * act * (1.0 + lax.erf(act * (2.0 ** -0.5)))).astype(BF16)
        n_b = [packed_row(n_rows[hh], ii) for hh in range(PEER_HEADS)]
        w_b = [packed_row(w_rows[hh], ii) for hh in range(PEER_HEADS)]
        for c in range(PEER_KEYS // BF16_ROWS):
            js = slice(c * BF16_ROWS, (c + 1) * BF16_ROWS)
            gate = jnp.zeros((BF16_ROWS, tok), BF16)
            for hh in range(PEER_HEADS):
                e2 = e2_ref[0, hh, js, :]
                gate = gate + jnp.where(r2_ref[0, hh, js, :] < n_b[hh], e2 * w_b[hh],
                                        jnp.zeros_like(e2))
            r = ii * PEER_KEYS + c * BF16_ROWS
            coef_ref[r:r + BF16_ROWS, :] = gate * act[js, :]
    acc_ref[...] += jnp.dot(vt_ref[0], coef_ref[...], preferred_element_type=F32)

    @pl.when(e == ne - 1)
    def _():
        y = DEEPNORM_ALPHA * h_ref[...] + acc_ref[...].T
        out_ref[...] = _layer_norm(y, g2_ref[...], b2_ref[...])


def _peer(rows, h1, u_bf, vt_bf, r2, e2, n1, w1, g2, b2):
    tok = PEER_TOK
    sel = pl.BlockSpec((1, PEER_HEADS, PEER_KEYS, tok), lambda t, e: (t, 0, 0, 0))
    return pl.pallas_call(
        _peer_kernel,
        grid=(rows // tok, PEER_N // PEER_EXP),
        in_specs=[
            pl.BlockSpec((tok, D_MODEL), lambda t, e: (t, 0)),
            pl.BlockSpec((PEER_EXP, D_MODEL), lambda t, e: (e, 0)),
            pl.BlockSpec((1, D_MODEL, PEER_EXP), lambda t, e: (e, 0, 0)),
            sel, sel, sel, sel,
            pl.BlockSpec((1, D_MODEL), lambda t, e: (0, 0)),
            pl.BlockSpec((1, D_MODEL), lambda t, e: (0, 0)),
        ],
        out_specs=pl.BlockSpec((tok, D_MODEL), lambda t, e: (t, 0)),
        out_shape=jax.ShapeDtypeStruct((rows, D_MODEL), F32),
        scratch_shapes=[
            pltpu.VMEM((D_MODEL, tok), BF16),
            pltpu.VMEM((D_MODEL, tok), F32),
            pltpu.VMEM((PEER_EXP, tok), BF16),
        ],
        compiler_params=_params(("arbitrary", "arbitrary")),
        name="peer",
    )(h1, u_bf, vt_bf, r2, e2, n1, w1, g2, b2)


def kernel(x, meta, ln_in_g, ln_in_b, w_in, b_in, lambda_q1, lambda_k1, lambda_q2, lambda_k2,
           subln_g, w_attn_proj, dw_kernel, dw_bias, conv_ln_g, conv_ln_b, w_conv_proj,
           b_conv_proj, w_out, b_out, ln1_g, ln1_b, peer_wq, peer_subkeys, peer_u, peer_v,
           ln2_g, ln2_b):
    nb, seq, _ = x.shape
    rows = nb * seq
    assert w_in.shape[0] == DEPTH and seq % PEER_TOK == 0 and PEER_TOK == ROUTE_TILE
    vec = lambda a: a.reshape(1, -1)
    x2 = x.reshape(rows, D_MODEL)
    w_bf = w_in[0].astype(BF16)
    bias = vec(b_in[0])
    g_in, b_ln = vec(ln_in_g), vec(ln_in_b)

    cos_t, sa_t, sb_t = _rope_tables(N_META + seq)
    frame_tabs = [t[N_META:] for t in (cos_t, sa_t, sb_t)]
    meta_tabs = [t[:META_PAD] for t in (cos_t, sa_t, sb_t)]
    meta_rows = jnp.pad(meta, ((0, META_PAD - N_META), (0, 0)))

    q, k, vt, u, sg = _in_proj(rows, seq, IN_PROJ_TILE, x2, g_in, b_ln, w_bf, bias, *frame_tabs)
    _, k_meta, vt_meta, u_meta, _ = _in_proj(META_PAD, META_PAD, META_PAD, meta_rows, g_in, b_ln,
                                             w_bf, bias, *meta_tabs)

    o = _attention(nb, seq, q, k, vt, k_meta, vt_meta, vec(lambda_q1[0]), vec(lambda_k1[0]),
                   vec(lambda_q2[0]), vec(lambda_k2[0]), vec(subln_g[0]))
    dw = jnp.pad(dw_kernel[0], ((0, 1), (0, 0)))
    cu = _conv(nb, seq, u, u_meta, dw, vec(dw_bias[0]), vec(conv_ln_g[0]), vec(conv_ln_b[0]))
    h1 = _merge(rows, x2, o, cu, sg, g_in, b_ln, w_attn_proj[0].astype(BF16),
                w_conv_proj[0].astype(BF16), vec(b_conv_proj[0]), w_out[0].astype(BF16),
                vec(b_out[0]), vec(ln1_g[0]), vec(ln1_b[0]))

    wqt = peer_wq[0].T.astype(BF16)
    sk = peer_subkeys[0].reshape(2 * PEER_HEADS, PEER_KEYS, PEER_DKEY // 2).astype(BF16)
    r2, e2, n1, w1 = _route(rows, h1, wqt, sk)
    vt_blocks = (peer_v[0].astype(BF16).reshape(PEER_N // PEER_EXP, PEER_EXP, D_MODEL)
                 .transpose(0, 2, 1))
    h2 = _peer(rows, h1, peer_u[0].astype(BF16), vt_blocks, r2, e2, n1, w1,
               vec(ln2_g[0]), vec(ln2_b[0]))
    return h2.reshape(nb, seq, D_MODEL)
```
